```python
import math
import jax, jax.numpy as jnp
from jax import lax
import numpy as np

D_MODEL = 2048
BATCH = 16
SEQ = 2048
DEPTH = 1

LRU_W = D_MODEL // 2
LRU_BLOCKS = 8
LRU_BW = LRU_W // LRU_BLOCKS
CONV_W = 4
LRU_C = 8.0
ATT_DV = 128
ATT_DQK = ATT_DV // 2
ATT_HEADS = (D_MODEL - LRU_W) // ATT_DV
ATT_QK = 2 * ATT_HEADS * ATT_DQK
ATT_V = ATT_HEADS * ATT_DV
MIX_W = LRU_W + ATT_V
IN_W = 2 * LRU_W + 2 * ATT_QK + ATT_V
Q_BLOCK = 128
REL_BUCKETS = 32
REL_MAX_DIST = 128
FFN_HIDDEN = int(math.ceil(8 * D_MODEL / 3 / 256)) * 256
PLE_DIM = 256
LN_EPS = 1e-5
ALPHA = (2 * DEPTH) ** 0.25
BETA = (8 * DEPTH) ** -0.25

kernel_name = "hymba_rglru_diffattn_deepnorm_ple"


def layer_norm(x, g, b):
    xf = x.astype(jnp.float32)
    mu = jnp.mean(xf, axis=-1, keepdims=True)
    xc = xf - mu
    var = jnp.mean(xc * xc, axis=-1, keepdims=True)
    return (xc * lax.rsqrt(var + LN_EPS) * g.astype(jnp.float32) + b.astype(jnp.float32)).astype(x.dtype)


def rms_norm(x, g):
    xf = x.astype(jnp.float32)
    y = xf * lax.rsqrt(jnp.mean(xf * xf, axis=-1, keepdims=True) + LN_EPS)
    return y * g.astype(jnp.float32)


def t5_causal_bucket(rel):
    n = jnp.maximum(-rel, 0)
    max_exact = REL_BUCKETS // 2
    nf = jnp.maximum(n, 1).astype(jnp.float32)
    large = max_exact + (jnp.log(nf / max_exact) / math.log(REL_MAX_DIST / max_exact)
                         * (REL_BUCKETS - max_exact)).astype(jnp.int32)
    large = jnp.minimum(large, REL_BUCKETS - 1)
    return jnp.where(n < max_exact, n, large)


def causal_depthwise_conv(x, w, b):
    S = x.shape[1]
    xp = jnp.pad(x, ((0, 0), (CONV_W - 1, 0), (0, 0)))
    y = b
    for t in range(CONV_W):
        y = y + xp[:, t:t + S, :] * w[t]
    return y


def rg_lru(x, w_a, b_a, w_x, b_x, lam):
    B, S, W = x.shape
    xf = x.astype(jnp.float32)
    xb = xf.reshape(B, S, LRU_BLOCKS, LRU_BW)
    gate_x = jax.nn.sigmoid(jnp.einsum('bsnc,ncd->bsnd', xb, w_x.astype(jnp.float32)) + b_x.astype(jnp.float32)).reshape(B, S, W)
    gate_a = jax.nn.sigmoid(jnp.einsum('bsnc,ncd->bsnd', xb, w_a.astype(jnp.float32)) + b_a.astype(jnp.float32)).reshape(B, S, W)
    log_a = -LRU_C * gate_a * jax.nn.softplus(-lam.astype(jnp.float32))
    a = jnp.exp(log_a)
    mult = jnp.sqrt(-jnp.expm1(2.0 * log_a))
    u = mult * (gate_x * xf)

    def step(h, au):
        a_t, u_t = au
        h = a_t * h + u_t
        return h, h

    h0 = jnp.zeros((B, W), jnp.float32)
    _, hs = lax.scan(step, h0, (jnp.swapaxes(a, 0, 1), jnp.swapaxes(u, 0, 1)))
    return jnp.swapaxes(hs, 0, 1).astype(x.dtype)


def diff_attention(q, k, v, lq1, lk1, lq2, lk2, subln_g, rel_bias, lam_init):
    B, S, _ = q.shape
    H = ATT_HEADS
    q = q.reshape(B, S, 2 * H, ATT_DQK) * (ATT_DQK ** -0.5)
    k = k.reshape(B, S, 2 * H, ATT_DQK)
    v = v.reshape(B, S, H, ATT_DV)
    lam = (jnp.exp(jnp.sum(lq1.astype(jnp.float32) * lk1.astype(jnp.float32)))
           - jnp.exp(jnp.sum(lq2.astype(jnp.float32) * lk2.astype(jnp.float32))) + lam_init)
    nb = S // Q_BLOCK
    qb = jnp.swapaxes(q.reshape(B, nb, Q_BLOCK, 2 * H, ATT_DQK), 0, 1)
    starts = jnp.arange(nb, dtype=jnp.int32) * Q_BLOCK
    kpos = jnp.arange(S, dtype=jnp.int32)
    table = rel_bias.astype(jnp.float32)

    def one_block(args):
        q_blk, start = args
        qpos = start + jnp.arange(Q_BLOCK, dtype=jnp.int32)
        rel = kpos[None, :] - qpos[:, None]
        bias = jnp.transpose(table[t5_causal_bucket(rel)], (2, 0, 1))
        s = jnp.einsum('bqhd,bkhd->bhqk', q_blk, k).astype(jnp.float32)
        s = s.reshape(B, H, 2, Q_BLOCK, S) + bias[None, :, None]
        s = jnp.where((rel <= 0)[None, None, None], s, -jnp.inf)
        pr = jax.nn.softmax(s, axis=-1)
        attn = pr[:, :, 0] - lam * pr[:, :, 1]
        return jnp.einsum('bhqk,bkhd->bqhd', attn.astype(v.dtype), v)

    out = lax.map(one_block, (qb, starts))
    out = jnp.swapaxes(out, 0, 1).reshape(B, S, H, ATT_DV)
    out = rms_norm(out, subln_g) * (1.0 - lam_init)
    return out.reshape(B, S, ATT_V).astype(q.dtype)


def hybrid_mixer(h, w_in, conv_w, conv_b, lru_wa, lru_ba, lru_wx, lru_bx, lru_lambda,
                 lq1, lk1, lq2, lk2, subln_g, rel_bias, w_out, lam_init):
    proj = jnp.einsum('bsd,de->bse', h, w_in)
    xr, yg, q, k, v = jnp.split(
        proj, [LRU_W, 2 * LRU_W, 2 * LRU_W + ATT_QK, 2 * LRU_W + 2 * ATT_QK], axis=-1)
    rec = rg_lru(causal_depthwise_conv(xr, conv_w, conv_b), lru_wa, lru_ba, lru_wx, lru_bx, lru_lambda)
    rec = rec * jax.nn.gelu(yg, approximate=True)
    att = diff_attention(q, k, v, lq1, lk1, lq2, lk2, subln_g, rel_bias, lam_init)
    merged = jnp.concatenate([rec, att], axis=-1)
    return jnp.einsum('bse,ed->bsd', merged, w_out)


def swiglu(h, w_gate, w_up, w_down):
    g = jnp.einsum('bsd,df->bsf', h, w_gate)
    u = jnp.einsum('bsd,df->bsf', h, w_up)
    return jnp.einsum('bsf,fd->bsd', jax.nn.silu(g) * u, w_down)


def setup_inputs(seed: int = 0) -> dict:
    key = jax.random.key(seed)
    ks = jax.random.split(key, 32)
    L, D = DEPTH, D_MODEL

    def nrm(k, shape, scale):
        return jax.random.normal(k, shape, jnp.float32) * scale

    x = nrm(ks[0], (BATCH, SEQ, D), 1.0)
    p = nrm(ks[1], (L, BATCH, SEQ, PLE_DIM), 1.0)
    col_scale = jnp.concatenate([jnp.ones((IN_W - ATT_V,), jnp.float32),
                                 jnp.full((ATT_V,), BETA, jnp.float32)])
    w_in = nrm(ks[2], (L, D, IN_W), D ** -0.5) * col_scale
    conv_w = nrm(ks[3], (L, CONV_W, LRU_W), CONV_W ** -0.5)
    conv_b = nrm(ks[4], (L, LRU_W), 0.01)
    lru_wa = nrm(ks[5], (L, LRU_BLOCKS, LRU_BW, LRU_BW), LRU_BW ** -0.5)
    lru_ba = nrm(ks[6], (L, LRU_BLOCKS, LRU_BW), 0.01)
    lru_wx = nrm(ks[7], (L, LRU_BLOCKS, LRU_BW, LRU_BW), LRU_BW ** -0.5)
    lru_bx = nrm(ks[8], (L, LRU_BLOCKS, LRU_BW), 0.01)
    u = jax.random.uniform(ks[9], (L, LRU_W), jnp.float32, 0.9, 0.999)
    s = u ** (1.0 / LRU_C)
    lru_lambda = jnp.log(s) - jnp.log1p(-s)
    diff_lq1 = nrm(ks[10], (L, ATT_DQK), 0.1)
    diff_lk1 = nrm(ks[11], (L, ATT_DQK), 0.1)
    diff_lq2 = nrm(ks[12], (L, ATT_DQK), 0.1)
    diff_lk2 = nrm(ks[13], (L, ATT_DQK), 0.1)
    diff_subln_g = 1.0 + nrm(ks[14], (L, ATT_DV), 0.02)
    rel_bias = nrm(ks[15], (REL_BUCKETS, ATT_HEADS), 0.5)
    w_out = nrm(ks[16], (L, MIX_W, D), MIX_W ** -0.5 * BETA)
    ln1_g = 1.0 + nrm(ks[17], (L, D), 0.02)
    ln1_b = nrm(ks[18], (L, D), 0.01)
    w_ffn_gate = nrm(ks[19], (L, D, FFN_HIDDEN), D ** -0.5 * BETA)
    w_ffn_up = nrm(ks[20], (L, D, FFN_HIDDEN), D ** -0.5 * BETA)
    w_ffn_down = nrm(ks[21], (L, FFN_HIDDEN, D), FFN_HIDDEN ** -0.5 * BETA)
    ln2_g = 1.0 + nrm(ks[22], (L, D), 0.02)
    ln2_b = nrm(ks[23], (L, D), 0.01)
    w_ple_gate = nrm(ks[24], (L, D, D), D ** -0.5)
    b_ple_gate = nrm(ks[25], (L, D), 0.01)
    w_ple_proj = nrm(ks[26], (L, PLE_DIM, D), PLE_DIM ** -0.5 * BETA)
    ln3_g = 1.0 + nrm(ks[27], (L, D), 0.02)
    ln3_b = nrm(ks[28], (L, D), 0.01)
    return {"x": x, "p": p, "w_in": w_in, "conv_w": conv_w, "conv_b": conv_b,
            "lru_wa": lru_wa, "lru_ba": lru_ba, "lru_wx": lru_wx, "lru_bx": lru_bx,
            "lru_lambda": lru_lambda, "diff_lq1": diff_lq1, "diff_lk1": diff_lk1,
            "diff_lq2": diff_lq2, "diff_lk2": diff_lk2, "diff_subln_g": diff_subln_g,
            "rel_bias": rel_bias, "w_out": w_out, "ln1_g": ln1_g, "ln1_b": ln1_b,
            "w_ffn_gate": w_ffn_gate, "w_ffn_up": w_ffn_up, "w_ffn_down": w_ffn_down,
            "ln2_g": ln2_g, "ln2_b": ln2_b, "w_ple_gate": w_ple_gate, "b_ple_gate": b_ple_gate,
            "w_ple_proj": w_ple_proj, "ln3_g": ln3_g, "ln3_b": ln3_b}


def reference(x, p, w_in, conv_w, conv_b, lru_wa, lru_ba, lru_wx, lru_bx, lru_lambda,
              diff_lq1, diff_lk1, diff_lq2, diff_lk2, diff_subln_g, rel_bias, w_out,
              ln1_g, ln1_b, w_ffn_gate, w_ffn_up, w_ffn_down, ln2_g, ln2_b,
              w_ple_gate, b_ple_gate, w_ple_proj, ln3_g, ln3_b):
    h = x
    for i in range(DEPTH):
        lam_init = 0.8 - 0.6 * math.exp(-0.3 * i)
        m = hybrid_mixer(h, w_in[i], conv_w[i], conv_b[i], lru_wa[i], lru_ba[i], lru_wx[i], lru_bx[i],
                         lru_lambda[i], diff_lq1[i], diff_lk1[i], diff_lq2[i], diff_lk2[i],
                         diff_subln_g[i], rel_bias, w_out[i], lam_init)
        h = layer_norm(ALPHA * h + m, ln1_g[i], ln1_b[i])
        f = swiglu(h, w_ffn_gate[i], w_ffn_up[i], w_ffn_down[i])
        h = layer_norm(ALPHA * h + f, ln2_g[i], ln2_b[i])
        gate = jax.nn.sigmoid(jnp.einsum('bsd,de->bse', h, w_ple_gate[i]) + b_ple_gate[i])
        e = jnp.einsum('bsk,kd->bsd', p[i], w_ple_proj[i])
        h = layer_norm(ALPHA * h + gate * e, ln3_g[i], ln3_b[i])
    return h
```

```python
import functools
import math

import jax
import jax.numpy as jnp
import numpy as np
from jax import lax
from jax.experimental import pallas as pl
from jax.experimental.pallas import tpu as pltpu

F32 = jnp.float32
BF16 = jnp.bfloat16

LN_EPS = 1e-5
LRU_C = 8.0
CONV_W = 4
ATT_DV = 128
ATT_DQK = ATT_DV // 2
REL_BUCKETS = 32
REL_MAX_DIST = 128

V7X_LANES = 128
V7X_SUBLANES = 8
V7X_VMEM_BYTES = 64 * 1024 * 1024
V7X_VMEM_RESERVE_BYTES = 6 * 1024 * 1024


def _vmem_limit(estimate_bytes):
    return int(min(V7X_VMEM_BYTES - V7X_VMEM_RESERVE_BYTES, max(2 * estimate_bytes, 16 * 1024 * 1024)))


def _tile(dim, preferred):
    if dim % preferred == 0:
        return preferred
    return dim


def _layer_norm(y, g, b):
    mu = jnp.mean(y, axis=-1, keepdims=True)
    yc = y - mu
    var = jnp.mean(yc * yc, axis=-1, keepdims=True)
    return yc * lax.rsqrt(var + LN_EPS) * g + b


def _dot(a, b):
    return jnp.dot(a, b, preferred_element_type=F32)


def _in_proj_kernel(x_ref, w_ref, o_ref, xb_ref):
    @pl.when(pl.program_id(1) == 0)
    def _():
        xb_ref[...] = x_ref[...].astype(BF16)

    o_ref[...] = _dot(xb_ref[...], w_ref[...]).astype(o_ref.dtype)


def _in_proj(x2d, w_bf16):
    m, d = x2d.shape
    n = w_bf16.shape[1]
    tm, tn = _tile(m, 1024), _tile(n, 1024)
    est = 2 * tm * d * 4 + 2 * d * tn * 2 + 2 * tm * tn * 2 + tm * d * 2 + tm * tn * 4
    return pl.pallas_call(
        _in_proj_kernel,
        grid=(m // tm, n // tn),
        in_specs=[pl.BlockSpec((tm, d), lambda i, j: (i, 0)),
                  pl.BlockSpec((d, tn), lambda i, j: (0, j))],
        out_specs=pl.BlockSpec((tm, tn), lambda i, j: (i, j)),
        out_shape=jax.ShapeDtypeStruct((m, n), BF16),
        scratch_shapes=[pltpu.VMEM((tm, d), BF16)],
        compiler_params=pltpu.CompilerParams(
            dimension_semantics=("parallel", "arbitrary"), vmem_limit_bytes=_vmem_limit(est)),
        name="in_proj",
    )(x2d, w_bf16)


def _linear_scan(a, u):
    n = a.shape[0]
    row = lax.broadcasted_iota(jnp.int32, a.shape, 0)
    s = 1
    while s < n:
        a_prev = pltpu.roll(a, s, axis=0)
        u_prev = pltpu.roll(u, s, axis=0)
        valid = row >= s
        u = jnp.where(valid, a * u_prev + u, u)
        a = jnp.where(valid, a * a_prev, a)
        s *= 2
    return a, u


def _gelu_tanh(x):
    return 0.5 * x * (1.0 + jnp.tanh(math.sqrt(2.0 / math.pi) * (x + 0.044715 * (x * x * x))))


def _lru_kernel(xr_ref, yg_ref, cw_ref, cb_ref, wg_ref, ba_ref, bx_ref, lam_ref, o_ref, xpad_ref, h_ref):
    ts = xr_ref.shape[1]
    width = xr_ref.shape[2]
    bw = wg_ref.shape[1]
    halo = V7X_SUBLANES

    @pl.when(pl.program_id(1) == 0)
    def _():
        xpad_ref[0:halo, :] = jnp.zeros((halo, width), F32)
        h_ref[...] = jnp.zeros(h_ref.shape, F32)

    xpad_ref[halo:halo + ts, :] = xr_ref[0].astype(F32)

    for n in range(width // bw):
        cols = slice(n * bw, (n + 1) * bw)
        xc = cb_ref[:, cols]
        for t in range(CONV_W):
            start = halo - (CONV_W - 1) + t
            xc = xc + xpad_ref[start:start + ts, cols] * cw_ref[t:t + 1, cols]
        gates = _dot(xc.astype(BF16), wg_ref[n])
        gate_a = jax.nn.sigmoid(gates[:, :bw] + ba_ref[:, cols])
        gate_x = jax.nn.sigmoid(gates[:, bw:] + bx_ref[:, cols])
        neg_lam = -lam_ref[:, cols]
        softplus = jnp.maximum(neg_lam, 0.0) + jnp.log1p(jnp.exp(-jnp.abs(neg_lam)))
        log_a = -LRU_C * gate_a * softplus
        a = jnp.exp(log_a)
        mult = jnp.sqrt(-jnp.tanh(log_a) * (a * a + 1.0))
        u = mult * (gate_x * xc)
        a_cum, h_loc = _linear_scan(a, u)
        h = h_loc + a_cum * h_ref[0:1, cols]
        h_ref[0:1, cols] = h[ts - 1:ts, :]
        o_ref[0, :, cols] = (h * _gelu_tanh(yg_ref[0, :, cols].astype(F32))).astype(o_ref.dtype)

    xpad_ref[0:halo, :] = xpad_ref[ts:ts + halo, :]


def _lru_branch(proj3d, conv_w, conv_b, wgate, ba, bx, lam, lru_w):
    b, s, _ = proj3d.shape
    nblk, bw, _ = wgate.shape
    ts = _tile(s, 256)
    est = 2 * 2 * ts * lru_w * 2 + 2 * ts * lru_w * 2 + (ts + 8) * lru_w * 4 + 2 * nblk * bw * 2 * bw * 2
    row = lambda bi, ti: (0, 0)
    return pl.pallas_call(
        _lru_kernel,
        grid=(b, s // ts),
        in_specs=[pl.BlockSpec((1, ts, lru_w), lambda bi, ti: (bi, ti, 0)),
                  pl.BlockSpec((1, ts, lru_w), lambda bi, ti: (bi, ti, 1)),
                  pl.BlockSpec((CONV_W, lru_w), row),
                  pl.BlockSpec((1, lru_w), row),
                  pl.BlockSpec((nblk, bw, 2 * bw), lambda bi, ti: (0, 0, 0)),
                  pl.BlockSpec((1, lru_w), row),
                  pl.BlockSpec((1, lru_w), row),
                  pl.BlockSpec((1, lru_w), row)],
        out_specs=pl.BlockSpec((1, ts, lru_w), lambda bi, ti: (bi, ti, 0)),
        out_shape=jax.ShapeDtypeStruct((b, s, lru_w), BF16),
        scratch_shapes=[pltpu.VMEM((ts + V7X_SUBLANES, lru_w), F32),
                        pltpu.VMEM((V7X_SUBLANES, lru_w), F32)],
        compiler_params=pltpu.CompilerParams(
            dimension_semantics=("parallel", "arbitrary"), vmem_limit_bytes=_vmem_limit(est)),
        name="rg_lru",
    )(proj3d, proj3d, conv_w, conv_b, wgate, ba, bx, lam)


def _bucket_thresholds():
    n = np.arange(0, 8 * REL_MAX_DIST)
    max_exact = REL_BUCKETS // 2
    nf = np.maximum(n, 1).astype(np.float32)
    large = max_exact + (np.log(nf / np.float32(max_exact)) / np.float32(math.log(REL_MAX_DIST / max_exact))
                         * np.float32(REL_BUCKETS - max_exact)).astype(np.int32)
    bucket = np.where(n < max_exact, n, np.minimum(large, REL_BUCKETS - 1))
    assert np.all(np.diff(bucket) >= 0) and bucket[-1] == REL_BUCKETS - 1
    return [int(np.argmax(bucket >= j)) for j in range(1, REL_BUCKETS)]


def _bias_tile_bases(tq, tk):
    step = math.gcd(tq, tk)
    last = _bucket_thresholds()[-1]
    return [bse for bse in range(0, tk + last, step) if bse - (tk - 1) < last]


def _bias_kernel(table_ref, o_ref, *, bases, thresholds):
    h = pl.program_id(0)
    tq, tk = o_ref.shape[2], o_ref.shape[3]
    r = lax.broadcasted_iota(jnp.int32, (tq, tk), 0)
    c = lax.broadcasted_iota(jnp.int32, (tq, tk), 1)
    for i, base in enumerate(bases):
        dist = base + r - c
        val = jnp.full((tq, tk), table_ref[0, h], F32)
        for j, thr in enumerate(thresholds):
            val = jnp.where(dist >= thr, table_ref[j + 1, h], val)
        o_ref[0, i] = jnp.where(dist < 0, -jnp.inf, val)
    o_ref[0, len(bases)] = jnp.full((tq, tk), table_ref[REL_BUCKETS - 1, h], F32)


def _bias_tiles(rel_bias, tq, tk):
    heads = rel_bias.shape[1]
    bases = _bias_tile_bases(tq, tk)
    nt = len(bases) + 1
    return pl.pallas_call(
        functools.partial(_bias_kernel, bases=bases, thresholds=_bucket_thresholds()),
        grid=(heads,),
        in_specs=[pl.BlockSpec(memory_space=pltpu.SMEM)],
        out_specs=pl.BlockSpec((1, nt, tq, tk), lambda h: (h, 0, 0, 0)),
        out_shape=jax.ShapeDtypeStruct((heads, nt, tq, tk), F32),
        compiler_params=pltpu.CompilerParams(
            dimension_semantics=("parallel",), vmem_limit_bytes=_vmem_limit(2 * nt * tq * tk * 4)),
        name="rel_bias_tiles",
    )(rel_bias)


def _attn_kernel(q_ref, k_ref, v_ref, bias_ref, lqk_ref, g_ref, o_ref, *, tk, step, lam_init):
    qi = pl.program_id(2)
    tq = q_ref.shape[1]
    n_tiles = bias_ref.shape[1]

    q = q_ref[0]
    lane = lax.broadcasted_iota(jnp.int32, q.shape, 1)
    scale = ATT_DQK ** -0.5
    zero = jnp.zeros_like(q)
    qs = (q * scale).astype(BF16)
    q_parts = (jnp.where(lane < ATT_DQK, qs, zero), jnp.where(lane >= ATT_DQK, qs, zero))

    def kv_step(j, carry):
        k = k_ref[0, pl.ds(pl.multiple_of(j * tk, tk), tk), :]
        v = v_ref[0, pl.ds(pl.multiple_of(j * tk, tk), tk), :]
        tile = jnp.minimum((qi * tq - j * tk) // step, n_tiles - 1)
        bias = bias_ref[0, tile]
        out = []
        for c in range(2):
            m_prev, l_prev, acc_prev = carry[c]
            s = lax.dot_general(q_parts[c], k, (((1,), (1,)), ((), ())), preferred_element_type=F32) + bias
            m_new = jnp.maximum(m_prev, jnp.max(s, axis=-1, keepdims=True))
            alpha = jnp.exp(m_prev - m_new)
            p = jnp.exp(s - m_new)
            l_new = alpha * l_prev + jnp.sum(p, axis=-1, keepdims=True)
            acc_new = alpha * acc_prev + _dot(p.astype(BF16), v)
            out.append((m_new, l_new, acc_new))
        return tuple(out)

    init = tuple((jnp.full((tq, 1), -jnp.inf, F32), jnp.zeros((tq, 1), F32), jnp.zeros((tq, ATT_DV), F32))
                 for _ in range(2))
    n_kv = (qi * tq + tq - 1) // tk + 1
    (_, l1, acc1), (_, l2, acc2) = lax.fori_loop(0, n_kv, kv_step, init)

    lqk = lqk_ref[...]
    lam = (jnp.exp(jnp.sum(lqk[0:1] * lqk[1:2], axis=-1, keepdims=True))
           - jnp.exp(jnp.sum(lqk[2:3] * lqk[3:4], axis=-1, keepdims=True)) + lam_init)
    out = acc1 / l1 - lam * (acc2 / l2)
    rms = lax.rsqrt(jnp.mean(out * out, axis=-1, keepdims=True) + LN_EPS)
    o_ref[0] = (out * rms * g_ref[...] * (1.0 - lam_init)).astype(o_ref.dtype)


def _diff_attention(proj3d, bias_tiles, lqk, subln_g, *, heads, qkv_col0, tq, tk, lam_init):
    b, s, _ = proj3d.shape
    nt = bias_tiles.shape[1]
    q_blk0 = qkv_col0 // ATT_DV
    k_blk0 = q_blk0 + heads
    v_blk0 = k_blk0 + heads
    est = 2 * tq * ATT_DV * 2 * 2 + 2 * 2 * s * ATT_DV * 2 + 2 * nt * tq * tk * 4 + 6 * tq * tk * 4
    kern = functools.partial(_attn_kernel, tk=tk, step=math.gcd(tq, tk), lam_init=lam_init)
    return pl.pallas_call(
        kern,
        grid=(b, heads, s // tq),
        in_specs=[pl.BlockSpec((1, tq, ATT_DV), lambda bi, h, qi: (bi, qi, q_blk0 + h)),
                  pl.BlockSpec((1, s, ATT_DV), lambda bi, h, qi: (bi, 0, k_blk0 + h)),
                  pl.BlockSpec((1, s, ATT_DV), lambda bi, h, qi: (bi, 0, v_blk0 + h)),
                  pl.BlockSpec((1, nt, tq, tk), lambda bi, h, qi: (h, 0, 0, 0)),
                  pl.BlockSpec(lqk.shape, lambda bi, h, qi: (0, 0)),
                  pl.BlockSpec((1, ATT_DV), lambda bi, h, qi: (0, 0))],
        out_specs=pl.BlockSpec((1, tq, ATT_DV), lambda bi, h, qi: (bi, qi, h)),
        out_shape=jax.ShapeDtypeStruct((b, s, heads * ATT_DV), BF16),
        compiler_params=pltpu.CompilerParams(
            dimension_semantics=("parallel", "parallel", "arbitrary"), vmem_limit_bytes=_vmem_limit(est)),
        name="diff_attention",
    )(proj3d, proj3d, proj3d, bias_tiles, lqk, subln_g)


def _out_proj_kernel(x_ref, rec_ref, att_ref, w_ref, g_ref, b_ref, o_ref, *, alpha):
    w1 = rec_ref.shape[1]
    m = _dot(rec_ref[...], w_ref[0:w1, :]) + _dot(att_ref[...], w_ref[w1:, :])
    o_ref[...] = _layer_norm(alpha * x_ref[...] + m, g_ref[...], b_ref[...])


def _out_proj_ln(x2d, rec2d, att2d, w_bf16, g, b, alpha):
    m, d = x2d.shape
    w1, w2 = rec2d.shape[1], att2d.shape[1]
    tm = _tile(m, 512)
    est = 2 * tm * d * 4 * 2 + 2 * tm * (w1 + w2) * 2 + 2 * (w1 + w2) * d * 2 + 2 * tm * d * 4
    rowblk = lambda i: (i, 0)
    fixed = lambda i: (0, 0)
    return pl.pallas_call(
        functools.partial(_out_proj_kernel, alpha=alpha),
        grid=(m // tm,),
        in_specs=[pl.BlockSpec((tm, d), rowblk), pl.BlockSpec((tm, w1), rowblk), pl.BlockSpec((tm, w2), rowblk),
                  pl.BlockSpec((w1 + w2, d), fixed), pl.BlockSpec((1, d), fixed), pl.BlockSpec((1, d), fixed)],
        out_specs=pl.BlockSpec((tm, d), rowblk),
        out_shape=jax.ShapeDtypeStruct((m, d), F32),
        compiler_params=pltpu.CompilerParams(
            dimension_semantics=("parallel",), vmem_limit_bytes=_vmem_limit(est)),
        name="out_proj_ln1",
    )(x2d, rec2d, att2d, w_bf16, g, b)


def _ffn_kernel(h_ref, wg_ref, wu_ref, wd_ref, g_ref, b_ref, o_ref, hb_ref, acc_ref, *, alpha):
    j = pl.program_id(1)

    @pl.when(j == 0)
    def _():
        hb_ref[...] = h_ref[...].astype(BF16)
        acc_ref[...] = jnp.zeros(acc_ref.shape, F32)

    hb = hb_ref[...]
    gate = _dot(hb, wg_ref[...])
    up = _dot(hb, wu_ref[...])
    act = (gate * jax.nn.sigmoid(gate)) * up
    acc_ref[...] += _dot(act.astype(BF16), wd_ref[...])

    @pl.when(j == pl.num_programs(1) - 1)
    def _():
        o_ref[...] = _layer_norm(alpha * h_ref[...] + acc_ref[...], g_ref[...], b_ref[...])


def _ffn_ln(h2d, wg, wu, wd, g, b, alpha):
    m, d = h2d.shape
    f = wg.shape[1]
    tm, tf = _tile(m, 512), _tile(f, 512)
    est = 2 * tm * d * 4 * 2 + 2 * 3 * d * tf * 2 + tm * d * 2 + tm * d * 4 + 3 * tm * tf * 4
    rowblk = lambda i, j: (i, 0)
    fixed = lambda i, j: (0, 0)
    return pl.pallas_call(
        functools.partial(_ffn_kernel, alpha=alpha),
        grid=(m // tm, f // tf),
        in_specs=[pl.BlockSpec((tm, d), rowblk),
                  pl.BlockSpec((d, tf), lambda i, j: (0, j)),
                  pl.BlockSpec((d, tf), lambda i, j: (0, j)),
                  pl.BlockSpec((tf, d), lambda i, j: (j, 0)),
                  pl.BlockSpec((1, d), fixed), pl.BlockSpec((1, d), fixed)],
        out_specs=pl.BlockSpec((tm, d), rowblk),
        out_shape=jax.ShapeDtypeStruct((m, d), F32),
        scratch_shapes=[pltpu.VMEM((tm, d), BF16), pltpu.VMEM((tm, d), F32)],
        compiler_params=pltpu.CompilerParams(
            dimension_semantics=("parallel", "arbitrary"), vmem_limit_bytes=_vmem_limit(est)),
        name="swiglu_ln2",
    )(h2d, wg, wu, wd, g, b)


def _ple_kernel(h_ref, p_ref, wg_ref, bg_ref, wp_ref, g_ref, b_ref, o_ref, *, alpha):
    h = h_ref[...]
    gate = jax.nn.sigmoid(_dot(h.astype(BF16), wg_ref[...]) + bg_ref[...])
    e = _dot(p_ref[...].astype(BF16), wp_ref[...])
    o_ref[...] = _layer_norm(alpha * h + gate * e, g_ref[...], b_ref[...])


def _ple_ln(h2d, p2d, wg, bg, wp, g, b, alpha):
    m, d = h2d.shape
    pd = p2d.shape[1]
    tm = _tile(m, 512)
    est = 2 * tm * d * 4 * 2 + 2 * tm * pd * 4 + 2 * d * d * 2 + 2 * pd * d * 2 + 2 * tm * d * 4
    rowblk = lambda i: (i, 0)
    fixed = lambda i: (0, 0)
    return pl.pallas_call(
        functools.partial(_ple_kernel, alpha=alpha),
        grid=(m // tm,),
        in_specs=[pl.BlockSpec((tm, d), rowblk), pl.BlockSpec((tm, pd), rowblk),
                  pl.BlockSpec((d, d), fixed), pl.BlockSpec((1, d), fixed), pl.BlockSpec((pd, d), fixed),
                  pl.BlockSpec((1, d), fixed), pl.BlockSpec((1, d), fixed)],
        out_specs=pl.BlockSpec((tm, d), rowblk),
        out_shape=jax.ShapeDtypeStruct((m, d), F32),
        compiler_params=pltpu.CompilerParams(
            dimension_semantics=("parallel",), vmem_limit_bytes=_vmem_limit(est)),
        name="ple_ln3",
    )(h2d, p2d, wg, bg, wp, g, b)


def kernel(x, p, w_in, conv_w, conv_b, lru_wa, lru_ba, lru_wx, lru_bx, lru_lambda, diff_lq1, diff_lk1, diff_lq2, diff_lk2, diff_subln_g, rel_bias, w_out, ln1_g, ln1_b, w_ffn_gate, w_ffn_up, w_ffn_down, ln2_g, ln2_b, w_ple_gate, b_ple_gate, w_ple_proj, ln3_g, ln3_b):
    batch, seq, d = x.shape
    depth = w_in.shape[0]
    nblk, bw = lru_wa.shape[1], lru_wa.shape[2]
    lru_w = nblk * bw
    heads = rel_bias.shape[1]
    att_qk = 2 * heads * ATT_DQK
    assert w_in.shape[2] == 2 * lru_w + 2 * att_qk + heads * ATT_DV
    assert lru_w % ATT_DV == 0 and att_qk == heads * ATT_DV
    m = batch * seq
    alpha = (2 * depth) ** 0.25

    tq = _tile(seq, 256)
    tk = _tile(seq, 256)
    bias_tiles = _bias_tiles(rel_bias, tq, tk)
    row = lambda v: v.reshape(1, -1)

    h = x.reshape(m, d)
    for i in range(depth):
        lam_init = 0.8 - 0.6 * math.exp(-0.3 * i)
        proj = _in_proj(h, w_in[i].astype(BF16)).reshape(batch, seq, -1)
        wgate = jnp.concatenate([lru_wa[i], lru_wx[i]], axis=-1).astype(BF16)
        rec = _lru_branch(proj, conv_w[i], row(conv_b[i]), wgate, row(lru_ba[i]), row(lru_bx[i]),
                          row(lru_lambda[i]), lru_w)
        lqk = jnp.stack([diff_lq1[i], diff_lk1[i], diff_lq2[i], diff_lk2[i]])
        att = _diff_attention(proj, bias_tiles, lqk, row(diff_subln_g[i]), heads=heads, qkv_col0=2 * lru_w,
                              tq=tq, tk=tk, lam_init=lam_init)
        h = _out_proj_ln(h, rec.reshape(m, lru_w), att.reshape(m, heads * ATT_DV), w_out[i].astype(BF16),
                         row(ln1_g[i]), row(ln1_b[i]), alpha)
        h = _ffn_ln(h, w_ffn_gate[i].astype(BF16), w_ffn_up[i].astype(BF16), w_ffn_down[i].astype(BF16),
                    row(ln2_g[i]), row(ln2_b[i]), alpha)
        h = _ple_ln(h, p[i].reshape(m, -1), w_ple_gate[i].astype(BF16), row(b_ple_gate[i]),
                    w_ple_proj[i].astype(BF16), row(ln3_g[i]), row(ln3_b[i]), alpha)
    return h.reshape(batch, seq, d)
```

```python
import functools
import math

import jax
import jax.numpy as jnp
import numpy as np
from jax import lax
from jax.experimental import pallas as pl
from jax.experimental.pallas import tpu as pltpu

F32 = jnp.float32
BF16 = jnp.bfloat16

LN_EPS = 1e-5
LRU_C = 8.0
CONV_W = 4
ATT_DV = 128
ATT_DQK = ATT_DV // 2
REL_BUCKETS = 32
REL_MAX_DIST = 128

V7X_LANES = 128
V7X_SUBLANES = 8
V7X_VMEM_BYTES = 64 * 1024 * 1024
V7X_VMEM_RESERVE_BYTES = 6 * 1024 * 1024


def _vmem_limit(estimate_bytes):
    return int(min(V7X_VMEM_BYTES - V7X_VMEM_RESERVE_BYTES, max(2 * estimate_bytes, 16 * 1024 * 1024)))


def _tile(dim, preferred):
    if dim % preferred == 0:
        return preferred
    return dim


def _layer_norm(y, g, b):
    mu = jnp.mean(y, axis=-1, keepdims=True)
    yc = y - mu
    var = jnp.mean(yc * yc, axis=-1, keepdims=True)
    return yc * lax.rsqrt(var + LN_EPS) * g + b


def _dot(a, b):
    return jnp.dot(a, b, preferred_element_type=F32)


def _in_proj_kernel(x_ref, w_ref, o_ref, xb_ref):
    @pl.when(pl.program_id(1) == 0)
    def _():
        xb_ref[...] = x_ref[...].astype(BF16)

    o_ref[...] = _dot(xb_ref[...], w_ref[...]).astype(o_ref.dtype)


def _in_proj(x2d, w_bf16):
    m, d = x2d.shape
    n = w_bf16.shape[1]
    tm, tn = _tile(m, 1024), _tile(n, 1024)
    est = 2 * tm * d * 4 + 2 * d * tn * 2 + 2 * tm * tn * 2 + tm * d * 2 + tm * tn * 4
    return pl.pallas_call(
        _in_proj_kernel,
        grid=(m // tm, n // tn),
        in_specs=[pl.BlockSpec((tm, d), lambda i, j: (i, 0)),
                  pl.BlockSpec((d, tn), lambda i, j: (0, j))],
        out_specs=pl.BlockSpec((tm, tn), lambda i, j: (i, j)),
        out_shape=jax.ShapeDtypeStruct((m, n), BF16),
        scratch_shapes=[pltpu.VMEM((tm, d), BF16)],
        compiler_params=pltpu.CompilerParams(
            dimension_semantics=("parallel", "arbitrary"), vmem_limit_bytes=_vmem_limit(est)),
        name="in_proj",
    )(x2d, w_bf16)


def _linear_scan(a, u):
    n = a.shape[0]
    row = lax.broadcasted_iota(jnp.int32, a.shape, 0)
    s = 1
    while s < n:
        a_prev = pltpu.roll(a, s, axis=0)
        u_prev = pltpu.roll(u, s, axis=0)
        valid = row >= s
        u = jnp.where(valid, a * u_prev + u, u)
        a = jnp.where(valid, a * a_prev, a)
        s *= 2
    return a, u


def _gelu_tanh(x):
    return 0.5 * x * (1.0 + jnp.tanh(math.sqrt(2.0 / math.pi) * (x + 0.044715 * (x * x * x))))


def _lru_kernel(xr_ref, yg_ref, cw_ref, cb_ref, wg_ref, ba_ref, bx_ref, lam_ref, o_ref, xpad_ref, h_ref):
    ts = xr_ref.shape[1]
    width = xr_ref.shape[2]
    bw = wg_ref.shape[1]
    halo = V7X_SUBLANES

    @pl.when(pl.program_id(1) == 0)
    def _():
        xpad_ref[0:halo, :] = jnp.zeros((halo, width), F32)
        h_ref[...] = jnp.zeros(h_ref.shape, F32)

    xpad_ref[halo:halo + ts, :] = xr_ref[0].astype(F32)

    for n in range(width // bw):
        cols = slice(n * bw, (n + 1) * bw)
        xc = cb_ref[:, cols]
        for t in range(CONV_W):
            start = halo - (CONV_W - 1) + t
            xc = xc + xpad_ref[start:start + ts, cols] * cw_ref[t:t + 1, cols]
        gates = _dot(xc.astype(BF16), wg_ref[n])
        gate_a = jax.nn.sigmoid(gates[:, :bw] + ba_ref[:, cols])
        gate_x = jax.nn.sigmoid(gates[:, bw:] + bx_ref[:, cols])
        neg_lam = -lam_ref[:, cols]
        softplus = jnp.maximum(neg_lam, 0.0) + jnp.log1p(jnp.exp(-jnp.abs(neg_lam)))
        log_a = -LRU_C * gate_a * softplus
        a = jnp.exp(log_a)
        mult = jnp.sqrt(-jnp.tanh(log_a) * (a * a + 1.0))
        u = mult * (gate_x * xc)
        a_cum, h_loc = _linear_scan(a, u)
        h = h_loc + a_cum * h_ref[0:1, cols]
        h_ref[0:1, cols] = h[ts - 1:ts, :]
        o_ref[0, :, cols] = (h * _gelu_tanh(yg_ref[0, :, cols].astype(F32))).astype(o_ref.dtype)

    xpad_ref[0:halo, :] = xpad_ref[ts:ts + halo, :]


def _lru_branch(proj3d, conv_w, conv_b, wgate, ba, bx, lam, lru_w):
    b, s, _ = proj3d.shape
    nblk, bw, _ = wgate.shape
    ts = _tile(s, 256)
    est = 2 * 2 * ts * lru_w * 2 + 2 * ts * lru_w * 2 + (ts + 8) * lru_w * 4 + 2 * nblk * bw * 2 * bw * 2
    row = lambda bi, ti: (0, 0)
    return pl.pallas_call(
        _lru_kernel,
        grid=(b, s // ts),
        in_specs=[pl.BlockSpec((1, ts, lru_w), lambda bi, ti: (bi, ti, 0)),
                  pl.BlockSpec((1, ts, lru_w), lambda bi, ti: (bi, ti, 1)),
                  pl.BlockSpec((CONV_W, lru_w), row),
                  pl.BlockSpec((1, lru_w), row),
                  pl.BlockSpec((nblk, bw, 2 * bw), lambda bi, ti: (0, 0, 0)),
                  pl.BlockSpec((1, lru_w), row),
                  pl.BlockSpec((1, lru_w), row),
                  pl.BlockSpec((1, lru_w), row)],
        out_specs=pl.BlockSpec((1, ts, lru_w), lambda bi, ti: (bi, ti, 0)),
        out_shape=jax.ShapeDtypeStruct((b, s, lru_w), BF16),
        scratch_shapes=[pltpu.VMEM((ts + V7X_SUBLANES, lru_w), F32),
                        pltpu.VMEM((V7X_SUBLANES, lru_w), F32)],
        compiler_params=pltpu.CompilerParams(
            dimension_semantics=("parallel", "arbitrary"), vmem_limit_bytes=_vmem_limit(est)),
        name="rg_lru",
    )(proj3d, proj3d, conv_w, conv_b, wgate, ba, bx, lam)


def _bucket_thresholds():
    n = np.arange(0, 8 * REL_MAX_DIST)
    max_exact = REL_BUCKETS // 2
    nf = np.maximum(n, 1).astype(np.float32)
    large = max_exact + (np.log(nf / np.float32(max_exact)) / np.float32(math.log(REL_MAX_DIST / max_exact))
                         * np.float32(REL_BUCKETS - max_exact)).astype(np.int32)
    bucket = np.where(n < max_exact, n, np.minimum(large, REL_BUCKETS - 1))
    assert np.all(np.diff(bucket) >= 0) and bucket[-1] == REL_BUCKETS - 1
    return [int(np.argmax(bucket >= j)) for j in range(1, REL_BUCKETS)]


def _near_key_blocks(tq):
    last = _bucket_thresholds()[-1]
    d = 0
    while d * tq - (tq - 1) < last:
        d += 1
    return d


def _bias_kernel(table_ref, o_ref, *, thresholds):
    h = pl.program_id(0)
    nb, tk, tq = o_ref.shape[1], o_ref.shape[2], o_ref.shape[3]
    key = lax.broadcasted_iota(jnp.int32, (tk, tq), 0)
    qry = lax.broadcasted_iota(jnp.int32, (tk, tq), 1)
    far = table_ref[REL_BUCKETS - 1, h]
    for d in range(nb):
        dist = d * tk + qry - key
        val = jnp.full((tk, tq), table_ref[0, h], F32)
        for j, thr in enumerate(thresholds):
            val = jnp.where(dist >= thr, table_ref[j + 1, h], val)
        o_ref[0, d] = jnp.where(dist < 0, -jnp.inf, val - far)


def _bias_tiles(rel_bias, tq):
    heads = rel_bias.shape[1]
    nb = _near_key_blocks(tq)
    return pl.pallas_call(
        functools.partial(_bias_kernel, thresholds=_bucket_thresholds()),
        grid=(heads,),
        in_specs=[pl.BlockSpec(memory_space=pltpu.SMEM)],
        out_specs=pl.BlockSpec((1, nb, tq, tq), lambda h: (h, 0, 0, 0)),
        out_shape=jax.ShapeDtypeStruct((heads, nb, tq, tq), F32),
        compiler_params=pltpu.CompilerParams(
            dimension_semantics=("parallel",), vmem_limit_bytes=_vmem_limit(2 * nb * tq * tq * 4)),
        name="rel_bias_tiles",
    )(rel_bias)


def _attn_kernel(q_ref, k_ref, v_ref, bias_ref, lqk_ref, g_ref, o_ref, *, tq, lam_init):
    seq = q_ref.shape[1]
    nb = bias_ref.shape[1]
    scale = ATT_DQK ** -0.5
    nt_dims = (((1,), (1,)), ((), ()))

    lqk = lqk_ref[...]
    lam = (jnp.exp(jnp.sum(lqk[0:1] * lqk[1:2], axis=-1, keepdims=True))
           - jnp.exp(jnp.sum(lqk[2:3] * lqk[3:4], axis=-1, keepdims=True)) + lam_init)
    v_t = v_ref[0].astype(F32).T.astype(BF16)
    lane = lax.broadcasted_iota(jnp.int32, (tq, ATT_DV), 1)

    for i in range(seq // tq):
        q = q_ref[0, i * tq:(i + 1) * tq, :] * scale
        n_keys = (i + 1) * tq
        k = k_ref[0, 0:n_keys, :]
        n_far = max(i + 1 - nb, 0)
        probs, inv_sums = [], []
        for c in range(2):
            q_c = jnp.where((lane >= ATT_DQK) if c else (lane < ATT_DQK), q, jnp.zeros_like(q))
            s_t = lax.dot_general(k, q_c, nt_dims, preferred_element_type=F32)
            parts = [s_t[0:n_far * tq]] if n_far else []
            for j in range(n_far, i + 1):
                parts.append(s_t[j * tq:(j + 1) * tq] + bias_ref[0, i - j])
            m = functools.reduce(jnp.maximum, [jnp.max(part, axis=0, keepdims=True) for part in parts])
            parts = [jnp.exp(part - m) for part in parts]
            total = functools.reduce(jnp.add, [jnp.sum(part, axis=0, keepdims=True) for part in parts])
            probs.append(parts)
            inv_sums.append(1.0 / total)
        r1, r2 = inv_sums[0], lam * inv_sums[1]
        attn_t = jnp.concatenate([(p1 * r1 - p2 * r2).astype(BF16) for p1, p2 in zip(*probs)], axis=0)
        out_t = _dot(v_t[:, 0:n_keys], attn_t)
        rms = lax.rsqrt(jnp.mean(out_t * out_t, axis=0, keepdims=True) + LN_EPS)
        o_ref[0, i * tq:(i + 1) * tq, :] = ((out_t * rms).T * g_ref[...] * (1.0 - lam_init)).astype(o_ref.dtype)


def _diff_attention(proj3d, bias_tiles, lqk, subln_g, *, heads, qkv_col0, tq, lam_init):
    b, s, _ = proj3d.shape
    nb = bias_tiles.shape[1]
    q_blk0 = qkv_col0 // ATT_DV
    k_blk0 = q_blk0 + heads
    v_blk0 = k_blk0 + heads
    live_q_blocks = 4
    est = 2 * 4 * s * ATT_DV * 2 + 2 * nb * tq * tq * 4 + live_q_blocks * 2 * 2 * s * tq * 4
    kern = functools.partial(_attn_kernel, tq=tq, lam_init=lam_init)
    return pl.pallas_call(
        kern,
        grid=(b, heads),
        in_specs=[pl.BlockSpec((1, s, ATT_DV), lambda bi, h: (bi, 0, q_blk0 + h)),
                  pl.BlockSpec((1, s, ATT_DV), lambda bi, h: (bi, 0, k_blk0 + h)),
                  pl.BlockSpec((1, s, ATT_DV), lambda bi, h: (bi, 0, v_blk0 + h)),
                  pl.BlockSpec((1, nb, tq, tq), lambda bi, h: (h, 0, 0, 0)),
                  pl.BlockSpec(lqk.shape, lambda bi, h: (0, 0)),
                  pl.BlockSpec((1, ATT_DV), lambda bi, h: (0, 0))],
        out_specs=pl.BlockSpec((1, s, ATT_DV), lambda bi, h: (bi, 0, h)),
        out_shape=jax.ShapeDtypeStruct((b, s, heads * ATT_DV), BF16),
        compiler_params=pltpu.CompilerParams(
            dimension_semantics=("parallel", "parallel"), vmem_limit_bytes=_vmem_limit(est)),
        name="diff_attention",
    )(proj3d, proj3d, proj3d, bias_tiles, lqk, subln_g)


def _out_proj_kernel(x_ref, rec_ref, att_ref, w_ref, g_ref, b_ref, o_ref, *, alpha):
    w1 = rec_ref.shape[1]
    m = _dot(rec_ref[...], w_ref[0:w1, :]) + _dot(att_ref[...], w_ref[w1:, :])
    o_ref[...] = _layer_norm(alpha * x_ref[...] + m, g_ref[...], b_ref[...])


def _out_proj_ln(x2d, rec2d, att2d, w_bf16, g, b, alpha):
    m, d = x2d.shape
    w1, w2 = rec2d.shape[1], att2d.shape[1]
    tm = _tile(m, 512)
    est = 2 * tm * d * 4 * 2 + 2 * tm * (w1 + w2) * 2 + 2 * (w1 + w2) * d * 2 + 2 * tm * d * 4
    rowblk = lambda i: (i, 0)
    fixed = lambda i: (0, 0)
    return pl.pallas_call(
        functools.partial(_out_proj_kernel, alpha=alpha),
        grid=(m // tm,),
        in_specs=[pl.BlockSpec((tm, d), rowblk), pl.BlockSpec((tm, w1), rowblk), pl.BlockSpec((tm, w2), rowblk),
                  pl.BlockSpec((w1 + w2, d), fixed), pl.BlockSpec((1, d), fixed), pl.BlockSpec((1, d), fixed)],
        out_specs=pl.BlockSpec((tm, d), rowblk),
        out_shape=jax.ShapeDtypeStruct((m, d), F32),
        compiler_params=pltpu.CompilerParams(
            dimension_semantics=("parallel",), vmem_limit_bytes=_vmem_limit(est)),
        name="out_proj_ln1",
    )(x2d, rec2d, att2d, w_bf16, g, b)


def _ffn_kernel(h_ref, wg_ref, wu_ref, wd_ref, g_ref, b_ref, o_ref, hb_ref, acc_ref, *, alpha):
    j = pl.program_id(1)

    @pl.when(j == 0)
    def _():
        hb_ref[...] = h_ref[...].astype(BF16)
        acc_ref[...] = jnp.zeros(acc_ref.shape, F32)

    hb = hb_ref[...]
    gate = _dot(hb, wg_ref[...])
    up = _dot(hb, wu_ref[...])
    act = (gate * jax.nn.sigmoid(gate)) * up
    acc_ref[...] += _dot(act.astype(BF16), wd_ref[...])

    @pl.when(j == pl.num_programs(1) - 1)
    def _():
        o_ref[...] = _layer_norm(alpha * h_ref[...] + acc_ref[...], g_ref[...], b_ref[...])


def _ffn_ln(h2d, wg, wu, wd, g, b, alpha):
    m, d = h2d.shape
    f = wg.shape[1]
    tm, tf = _tile(m, 512), _tile(f, 512)
    est = 2 * tm * d * 4 * 2 + 2 * 3 * d * tf * 2 + tm * d * 2 + tm * d * 4 + 3 * tm * tf * 4
    rowblk = lambda i, j: (i, 0)
    fixed = lambda i, j: (0, 0)
    return pl.pallas_call(
        functools.partial(_ffn_kernel, alpha=alpha),
        grid=(m // tm, f // tf),
        in_specs=[pl.BlockSpec((tm, d), rowblk),
                  pl.BlockSpec((d, tf), lambda i, j: (0, j)),
                  pl.BlockSpec((d, tf), lambda i, j: (0, j)),
                  pl.BlockSpec((tf, d), lambda i, j: (j, 0)),
                  pl.BlockSpec((1, d), fixed), pl.BlockSpec((1, d), fixed)],
        out_specs=pl.BlockSpec((tm, d), rowblk),
        out_shape=jax.ShapeDtypeStruct((m, d), F32),
        scratch_shapes=[pltpu.VMEM((tm, d), BF16), pltpu.VMEM((tm, d), F32)],
        compiler_params=pltpu.CompilerParams(
            dimension_semantics=("parallel", "arbitrary"), vmem_limit_bytes=_vmem_limit(est)),
        name="swiglu_ln2",
    )(h2d, wg, wu, wd, g, b)


def _ple_kernel(h_ref, p_ref, wg_ref, bg_ref, wp_ref, g_ref, b_ref, o_ref, *, alpha):
    h = h_ref[...]
    gate = jax.nn.sigmoid(_dot(h.astype(BF16), wg_ref[...]) + bg_ref[...])
    e = _dot(p_ref[...].astype(BF16), wp_ref[...])
    o_ref[...] = _layer_norm(alpha * h + gate * e, g_ref[...], b_ref[...])


def _ple_ln(h2d, p2d, wg, bg, wp, g, b, alpha):
    m, d = h2d.shape
    pd = p2d.shape[1]
    tm = _tile(m, 512)
    est = 2 * tm * d * 4 * 2 + 2 * tm * pd * 4 + 2 * d * d * 2 + 2 * pd * d * 2 + 2 * tm * d * 4
    rowblk = lambda i: (i, 0)
    fixed = lambda i: (0, 0)
    return pl.pallas_call(
        functools.partial(_ple_kernel, alpha=alpha),
        grid=(m // tm,),
        in_specs=[pl.BlockSpec((tm, d), rowblk), pl.BlockSpec((tm, pd), rowblk),
                  pl.BlockSpec((d, d), fixed), pl.BlockSpec((1, d), fixed), pl.BlockSpec((pd, d), fixed),
                  pl.BlockSpec((1, d), fixed), pl.BlockSpec((1, d), fixed)],
        out_specs=pl.BlockSpec((tm, d), rowblk),
        out_shape=jax.ShapeDtypeStruct((m, d), F32),
        compiler_params=pltpu.CompilerParams(
            dimension_semantics=("parallel",), vmem_limit_bytes=_vmem_limit(est)),
        name="ple_ln3",
    )(h2d, p2d, wg, bg, wp, g, b)


def kernel(x, p, w_in, conv_w, conv_b, lru_wa, lru_ba, lru_wx, lru_bx, lru_lambda, diff_lq1, diff_lk1, diff_lq2, diff_lk2, diff_subln_g, rel_bias, w_out, ln1_g, ln1_b, w_ffn_gate, w_ffn_up, w_ffn_down, ln2_g, ln2_b, w_ple_gate, b_ple_gate, w_ple_proj, ln3_g, ln3_b):
    batch, seq, d = x.shape
    depth = w_in.shape[0]
    nblk, bw = lru_wa.shape[1], lru_wa.shape[2]
    lru_w = nblk * bw
    heads = rel_bias.shape[1]
    att_qk = 2 * heads * ATT_DQK
    assert w_in.shape[2] == 2 * lru_w + 2 * att_qk + heads * ATT_DV
    assert lru_w % ATT_DV == 0 and att_qk == heads * ATT_DV
    m = batch * seq
    alpha = (2 * depth) ** 0.25

    tq = _tile(seq, 256)
    bias_tiles = _bias_tiles(rel_bias, tq)
    row = lambda v: v.reshape(1, -1)

    h = x.reshape(m, d)
    for i in range(depth):
        lam_init = 0.8 - 0.6 * math.exp(-0.3 * i)
        proj = _in_proj(h, w_in[i].astype(BF16)).reshape(batch, seq, -1)
        wgate = jnp.concatenate([lru_wa[i], lru_wx[i]], axis=-1).astype(BF16)
        rec = _lru_branch(proj, conv_w[i], row(conv_b[i]), wgate, row(lru_ba[i]), row(lru_bx[i]),
                          row(lru_lambda[i]), lru_w)
        lqk = jnp.stack([diff_lq1[i], diff_lk1[i], diff_lq2[i], diff_lk2[i]])
        att = _diff_attention(proj, bias_tiles, lqk, row(diff_subln_g[i]), heads=heads, qkv_col0=2 * lru_w,
                              tq=tq, lam_init=lam_init)
        h = _out_proj_ln(h, rec.reshape(m, lru_w), att.reshape(m, heads * ATT_DV), w_out[i].astype(BF16),
                         row(ln1_g[i]), row(ln1_b[i]), alpha)
        h = _ffn_ln(h, w_ffn_gate[i].astype(BF16), w_ffn_up[i].astype(BF16), w_ffn_down[i].astype(BF16),
                    row(ln2_g[i]), row(ln2_b[i]), alpha)
        h = _ple_ln(h, p[i].reshape(m, -1), w_ple_gate[i].astype(BF16), row(b_ple_gate[i]),
                    w_ple_proj[i].astype(BF16), row(ln3_g[i]), row(ln3_b[i]), alpha)
    return h.reshape(batch, seq, d)
```

```python
import functools
import math

import jax
import jax.numpy as jnp
import numpy as np
from jax import lax
from jax.experimental import pallas as pl
from jax.experimental.pallas import tpu as pltpu

F32 = jnp.float32
BF16 = jnp.bfloat16

LN_EPS = 1e-5
LRU_C = 8.0
CONV_W = 4
ATT_DV = 128
ATT_DQK = ATT_DV // 2
REL_BUCKETS = 32
REL_MAX_DIST = 128
LOG2_E = math.log2(math.e)

V7X_LANES = 128
V7X_SUBLANES = 8
BF16_SUBLANES = 2 * V7X_SUBLANES
V7X_VMEM_BYTES = 64 * 1024 * 1024
V7X_VMEM_RESERVE_BYTES = 6 * 1024 * 1024


def _vmem_limit(estimate_bytes):
    return int(min(V7X_VMEM_BYTES - V7X_VMEM_RESERVE_BYTES, max(2 * estimate_bytes, 16 * 1024 * 1024)))


def _tile(dim, preferred):
    if dim % preferred == 0:
        return preferred
    return dim


def _layer_norm(y, g, b):
    mu = jnp.mean(y, axis=-1, keepdims=True)
    yc = y - mu
    var = jnp.mean(yc * yc, axis=-1, keepdims=True)
    return yc * lax.rsqrt(var + LN_EPS) * g + b


def _dot(a, b):
    return jnp.dot(a, b, preferred_element_type=F32)


def _gelu_tanh(x):
    return 0.5 * x * (1.0 + jnp.tanh(math.sqrt(2.0 / math.pi) * (x + 0.044715 * (x * x * x))))


def _sigmoid(x):
    return 0.5 * jnp.tanh(0.5 * x) + 0.5


def _in_proj_lru_kernel(x_ref, w_ref, cs_ref, cw_ref, cb_ref, wg_ref, ba_ref, bx_ref, lam_ref,
                        qkv_ref, rec_ref, xext_ref, a_ref, u_ref, h_ref, *, chunk):
    nbatch, tt, d = x_ref.shape
    rows = nbatch * tt
    lru_w = rec_ref.shape[2]
    n_qkv = qkv_ref.shape[2]
    bw = wg_ref.shape[1]
    halo = V7X_SUBLANES
    pitch = xext_ref.shape[1]
    half = V7X_SUBLANES

    @pl.when(pl.program_id(0) == 0)
    def _():
        xext_ref[:, 0:halo, :] = jnp.zeros((nbatch, halo, lru_w), F32)
        h_ref[...] = jnp.zeros(h_ref.shape, F32)

    xb = x_ref[...].reshape(rows, d).astype(BF16)
    xext_ref[:, halo:halo + tt, :] = _dot(xb, w_ref[:, 0:lru_w]).reshape(nbatch, tt, lru_w)
    yg = _dot(xb, w_ref[:, lru_w:2 * lru_w])
    for c0 in range(0, n_qkv, chunk):
        val = _dot(xb, w_ref[:, 2 * lru_w + c0:2 * lru_w + c0 + chunk]) * cs_ref[:, c0:c0 + chunk]
        qkv_ref[:, :, c0:c0 + chunk] = val.astype(BF16).reshape(nbatch, tt, chunk)

    for n in range(lru_w // bw):
        cols = slice(n * bw, (n + 1) * bw)
        xc = cb_ref[:, cols]
        for t in range(CONV_W):
            start = halo - (CONV_W - 1) + t
            xc = xc + xext_ref[:, start:start + tt, cols].reshape(rows, bw) * cw_ref[t:t + 1, cols]
        gates = _dot(xc.astype(BF16), wg_ref[n])
        gate_a = _sigmoid(gates[:, :bw] + ba_ref[:, cols])
        gate_x = _sigmoid(gates[:, bw:] + bx_ref[:, cols])
        neg_lam = -lam_ref[:, cols]
        softplus = jnp.maximum(neg_lam, 0.0) + jnp.log1p(jnp.exp(-jnp.abs(neg_lam)))
        log_a = -LRU_C * gate_a * softplus
        a = jnp.exp(log_a)
        one_minus_a2 = -jnp.tanh(log_a) * (a * a + 1.0)
        mult = jnp.where(one_minus_a2 > 0.0, one_minus_a2 * lax.rsqrt(one_minus_a2), 0.0)
        u = mult * (gate_x * xc)
        for bi in range(nbatch):
            a_ref[n, bi * pitch:bi * pitch + tt, :] = a[bi * tt:(bi + 1) * tt]
            u_ref[n, bi * pitch:bi * pitch + tt, :] = u[bi * tt:(bi + 1) * tt]

        for g in range(nbatch // half):
            h = h_ref[g * half:(g + 1) * half, cols]
            for t in range(tt):
                step = pl.ds(g * half * pitch + t, half, stride=pitch)
                h = a_ref[n, step, :] * h + u_ref[n, step, :]
                u_ref[n, step, :] = h
            h_ref[g * half:(g + 1) * half, cols] = h

        hs = jnp.concatenate([u_ref[n, bi * pitch:bi * pitch + tt, :] for bi in range(nbatch)], axis=0)
        rec_ref[:, :, cols] = (hs * _gelu_tanh(yg[:, cols])).astype(BF16).reshape(nbatch, tt, bw)

    xext_ref[:, 0:halo, :] = xext_ref[:, tt:tt + halo, :]


def _in_proj_lru(x, w_bf16, qkv_scale, conv_w, conv_b, wgate, ba, bx, lam):
    nbatch, seq, d = x.shape
    nblk, bw, _ = wgate.shape
    lru_w = nblk * bw
    n_in = w_bf16.shape[1]
    n_qkv = n_in - 2 * lru_w
    tt = _tile(seq, 32)
    rows = nbatch * tt
    pitch = tt + V7X_SUBLANES
    chunk = _tile(n_qkv, 1024)
    assert nbatch % V7X_SUBLANES == 0 and tt % BF16_SUBLANES == 0
    est = (d * n_in * 2 + 2 * rows * d * 4 + 2 * rows * (n_qkv + lru_w) * 2 + 3 * nbatch * pitch * lru_w * 4
           + rows * d * 2 + 3 * rows * lru_w * 4 + 2 * rows * chunk * 4)
    fixed = lambda t: (0, 0)
    kern = functools.partial(_in_proj_lru_kernel, chunk=chunk)
    return pl.pallas_call(
        kern,
        grid=(seq // tt,),
        in_specs=[pl.BlockSpec((nbatch, tt, d), lambda t: (0, t, 0)),
                  pl.BlockSpec((d, n_in), fixed, pipeline_mode=pl.Buffered(1)),
                  pl.BlockSpec((1, n_qkv), fixed),
                  pl.BlockSpec((CONV_W, lru_w), fixed),
                  pl.BlockSpec((1, lru_w), fixed),
                  pl.BlockSpec((nblk, bw, 2 * bw), lambda t: (0, 0, 0)),
                  pl.BlockSpec((1, lru_w), fixed),
                  pl.BlockSpec((1, lru_w), fixed),
                  pl.BlockSpec((1, lru_w), fixed)],
        out_specs=[pl.BlockSpec((nbatch, tt, n_qkv), lambda t: (0, t, 0)),
                   pl.BlockSpec((nbatch, tt, lru_w), lambda t: (0, t, 0))],
        out_shape=[jax.ShapeDtypeStruct((nbatch, seq, n_qkv), BF16),
                   jax.ShapeDtypeStruct((nbatch, seq, lru_w), BF16)],
        scratch_shapes=[pltpu.VMEM((nbatch, pitch, lru_w), F32),
                        pltpu.VMEM((nblk, nbatch * pitch, bw), F32),
                        pltpu.VMEM((nblk, nbatch * pitch, bw), F32),
                        pltpu.VMEM((nbatch, lru_w), F32)],
        compiler_params=pltpu.CompilerParams(
            dimension_semantics=("arbitrary",), vmem_limit_bytes=_vmem_limit(est)),
        name="in_proj_rg_lru",
    )(x, w_bf16, qkv_scale, conv_w, conv_b, wgate, ba, bx, lam)


def _bucket_thresholds():
    n = np.arange(0, 8 * REL_MAX_DIST)
    max_exact = REL_BUCKETS // 2
    nf = np.maximum(n, 1).astype(np.float32)
    large = max_exact + (np.log(nf / np.float32(max_exact)) / np.float32(math.log(REL_MAX_DIST / max_exact))
                         * np.float32(REL_BUCKETS - max_exact)).astype(np.int32)
    bucket = np.where(n < max_exact, n, np.minimum(large, REL_BUCKETS - 1))
    assert np.all(np.diff(bucket) >= 0) and bucket[-1] == REL_BUCKETS - 1
    return [int(np.argmax(bucket >= j)) for j in range(1, REL_BUCKETS)]


def _near_key_blocks(tq):
    last = _bucket_thresholds()[-1]
    d = 0
    while d * tq - (tq - 1) < last:
        d += 1
    return d


def _bias_kernel(table_ref, o_ref, *, thresholds):
    h = pl.program_id(0)
    nb, tk, tq = o_ref.shape[1], o_ref.shape[2], o_ref.shape[3]
    key = lax.broadcasted_iota(jnp.int32, (tk, tq), 0)
    qry = lax.broadcasted_iota(jnp.int32, (tk, tq), 1)
    far = table_ref[REL_BUCKETS - 1, h]
    for d in range(nb):
        dist = d * tk + qry - key
        val = jnp.full((tk, tq), table_ref[0, h], F32)
        for j, thr in enumerate(thresholds):
            val = jnp.where(dist >= thr, table_ref[j + 1, h], val)
        o_ref[0, d] = jnp.where(dist < 0, -jnp.inf, (val - far) * LOG2_E)


def _bias_tiles(rel_bias, tq):
    heads = rel_bias.shape[1]
    nb = _near_key_blocks(tq)
    return pl.pallas_call(
        functools.partial(_bias_kernel, thresholds=_bucket_thresholds()),
        grid=(heads,),
        in_specs=[pl.BlockSpec(memory_space=pltpu.SMEM)],
        out_specs=pl.BlockSpec((1, nb, tq, tq), lambda h: (h, 0, 0, 0)),
        out_shape=jax.ShapeDtypeStruct((heads, nb, tq, tq), F32),
        compiler_params=pltpu.CompilerParams(
            dimension_semantics=("parallel",), vmem_limit_bytes=_vmem_limit(2 * nb * tq * tq * 4)),
        name="rel_bias_tiles",
    )(rel_bias)


def _attn_kernel(q_ref, k_ref, v_ref, bias_ref, lqk_ref, g_ref, o_ref, *, tq, lam_init):
    seq = q_ref.shape[1]
    nb = bias_ref.shape[1]
    nt_dims = (((1,), (1,)), ((), ()))

    lqk = lqk_ref[...]
    lam = (jnp.exp(jnp.sum(lqk[0:1] * lqk[1:2], axis=-1, keepdims=True))
           - jnp.exp(jnp.sum(lqk[2:3] * lqk[3:4], axis=-1, keepdims=True)) + lam_init)
    ones_rows = (lax.broadcasted_iota(jnp.int32, (BF16_SUBLANES, seq), 0) == 0).astype(F32)
    v_aug_t = jnp.concatenate([v_ref[0].astype(F32).T, ones_rows], axis=0).astype(BF16)
    lane = lax.broadcasted_iota(jnp.int32, (tq, ATT_DV), 1)

    def scores(i):
        q = q_ref[0, i * tq:(i + 1) * tq, :]
        k = k_ref[0, 0:(i + 1) * tq, :]
        n_far = max(i + 1 - nb, 0)
        out = []
        for c in range(2):
            q_c = jnp.where((lane >= ATT_DQK) if c else (lane < ATT_DQK), q, jnp.zeros_like(q))
            s_t = lax.dot_general(k, q_c, nt_dims, preferred_element_type=F32)
            parts = [s_t[0:n_far * tq]] if n_far else []
            for j in range(n_far, i + 1):
                parts.append(s_t[j * tq:(j + 1) * tq] + bias_ref[0, i - j])
            m = functools.reduce(jnp.maximum, [jnp.max(part, axis=0, keepdims=True) for part in parts])
            out.append((parts, m))
        return out

    n_q = seq // tq
    pending = scores(0)
    for i in range(n_q):
        current, pending = pending, (scores(i + 1) if i + 1 < n_q else None)
        n_keys = (i + 1) * tq
        probs = [jnp.concatenate([jnp.exp2(part - m).astype(BF16) for part in parts], axis=0) for parts, m in current]
        pv = _dot(v_aug_t[:, 0:n_keys], jnp.concatenate(probs, axis=1))
        num1, num2 = pv[0:ATT_DV, 0:tq], pv[0:ATT_DV, tq:]
        sum1, sum2 = pv[ATT_DV:ATT_DV + 1, 0:tq], pv[ATT_DV:ATT_DV + 1, tq:]
        out_t = num1 * (1.0 / sum1) - num2 * (lam / sum2)
        rms = lax.rsqrt(jnp.mean(out_t * out_t, axis=0, keepdims=True) + LN_EPS)
        o_ref[0, i * tq:(i + 1) * tq, :] = ((out_t * rms).T * g_ref[...] * (1.0 - lam_init)).astype(o_ref.dtype)


def _diff_attention(proj3d, bias_tiles, lqk, subln_g, *, heads, qkv_col0, tq, lam_init):
    b, s, _ = proj3d.shape
    nb = bias_tiles.shape[1]
    q_blk0 = qkv_col0 // ATT_DV
    k_blk0 = q_blk0 + heads
    v_blk0 = k_blk0 + heads
    live_q_blocks = 4
    est = 2 * 4 * s * ATT_DV * 2 + 2 * nb * tq * tq * 4 + live_q_blocks * 2 * 2 * s * tq * 4
    kern = functools.partial(_attn_kernel, tq=tq, lam_init=lam_init)
    return pl.pallas_call(
        kern,
        grid=(b, heads),
        in_specs=[pl.BlockSpec((1, s, ATT_DV), lambda bi, h: (bi, 0, q_blk0 + h)),
                  pl.BlockSpec((1, s, ATT_DV), lambda bi, h: (bi, 0, k_blk0 + h)),
                  pl.BlockSpec((1, s, ATT_DV), lambda bi, h: (bi, 0, v_blk0 + h)),
                  pl.BlockSpec((1, nb, tq, tq), lambda bi, h: (h, 0, 0, 0)),
                  pl.BlockSpec(lqk.shape, lambda bi, h: (0, 0)),
                  pl.BlockSpec((1, ATT_DV), lambda bi, h: (0, 0))],
        out_specs=pl.BlockSpec((1, s, ATT_DV), lambda bi, h: (bi, 0, h)),
        out_shape=jax.ShapeDtypeStruct((b, s, heads * ATT_DV), BF16),
        compiler_params=pltpu.CompilerParams(
            dimension_semantics=("parallel", "parallel"), vmem_limit_bytes=_vmem_limit(est)),
        name="diff_attention",
    )(proj3d, proj3d, proj3d, bias_tiles, lqk, subln_g)


def _out_proj_kernel(x_ref, rec_ref, att_ref, w_ref, g_ref, b_ref, o_ref, *, alpha):
    w1 = rec_ref.shape[1]
    m = _dot(rec_ref[...], w_ref[0:w1, :]) + _dot(att_ref[...], w_ref[w1:, :])
    o_ref[...] = _layer_norm(alpha * x_ref[...] + m, g_ref[...], b_ref[...])


def _out_proj_ln(x2d, rec2d, att2d, w_bf16, g, b, alpha):
    m, d = x2d.shape
    w1, w2 = rec2d.shape[1], att2d.shape[1]
    tm = _tile(m, 512)
    est = 2 * tm * d * 4 * 2 + 2 * tm * (w1 + w2) * 2 + 2 * (w1 + w2) * d * 2 + 2 * tm * d * 4
    rowblk = lambda i: (i, 0)
    fixed = lambda i: (0, 0)
    return pl.pallas_call(
        functools.partial(_out_proj_kernel, alpha=alpha),
        grid=(m // tm,),
        in_specs=[pl.BlockSpec((tm, d), rowblk), pl.BlockSpec((tm, w1), rowblk), pl.BlockSpec((tm, w2), rowblk),
                  pl.BlockSpec((w1 + w2, d), fixed), pl.BlockSpec((1, d), fixed), pl.BlockSpec((1, d), fixed)],
        out_specs=pl.BlockSpec((tm, d), rowblk),
        out_shape=jax.ShapeDtypeStruct((m, d), F32),
        compiler_params=pltpu.CompilerParams(
            dimension_semantics=("parallel",), vmem_limit_bytes=_vmem_limit(est)),
        name="out_proj_ln1",
    )(x2d, rec2d, att2d, w_bf16, g, b)


def _ffn_kernel(h_ref, wg_ref, wu_ref, wd_ref, g_ref, b_ref, o_ref, hb_ref, acc_ref, *, alpha):
    j = pl.program_id(1)

    @pl.when(j == 0)
    def _():
        hb_ref[...] = h_ref[...].astype(BF16)
        acc_ref[...] = jnp.zeros(acc_ref.shape, F32)

    hb = hb_ref[...]
    gate = _dot(hb, wg_ref[...])
    up = _dot(hb, wu_ref[...])
    act = (gate * jax.nn.sigmoid(gate)) * up
    acc_ref[...] += _dot(act.astype(BF16), wd_ref[...])

    @pl.when(j == pl.num_programs(1) - 1)
    def _():
        o_ref[...] = _layer_norm(alpha * h_ref[...] + acc_ref[...], g_ref[...], b_ref[...])


def _ffn_ln(h2d, wg, wu, wd, g, b, alpha):
    m, d = h2d.shape
    f = wg.shape[1]
    tm, tf = _tile(m, 512), _tile(f, 512)
    est = 2 * tm * d * 4 * 2 + 2 * 3 * d * tf * 2 + tm * d * 2 + tm * d * 4 + 3 * tm * tf * 4
    rowblk = lambda i, j: (i, 0)
    fixed = lambda i, j: (0, 0)
    return pl.pallas_call(
        functools.partial(_ffn_kernel, alpha=alpha),
        grid=(m // tm, f // tf),
        in_specs=[pl.BlockSpec((tm, d), rowblk),
                  pl.BlockSpec((d, tf), lambda i, j: (0, j)),
                  pl.BlockSpec((d, tf), lambda i, j: (0, j)),
                  pl.BlockSpec((tf, d), lambda i, j: (j, 0)),
                  pl.BlockSpec((1, d), fixed), pl.BlockSpec((1, d), fixed)],
        out_specs=pl.BlockSpec((tm, d), rowblk),
        out_shape=jax.ShapeDtypeStruct((m, d), F32),
        scratch_shapes=[pltpu.VMEM((tm, d), BF16), pltpu.VMEM((tm, d), F32)],
        compiler_params=pltpu.CompilerParams(
            dimension_semantics=("parallel", "arbitrary"), vmem_limit_bytes=_vmem_limit(est)),
        name="swiglu_ln2",
    )(h2d, wg, wu, wd, g, b)


def _ple_kernel(h_ref, p_ref, wg_ref, bg_ref, wp_ref, g_ref, b_ref, o_ref, *, alpha):
    h = h_ref[...]
    gate = jax.nn.sigmoid(_dot(h.astype(BF16), wg_ref[...]) + bg_ref[...])
    e = _dot(p_ref[...].astype(BF16), wp_ref[...])
    o_ref[...] = _layer_norm(alpha * h + gate * e, g_ref[...], b_ref[...])


def _ple_ln(h2d, p2d, wg, bg, wp, g, b, alpha):
    m, d = h2d.shape
    pd = p2d.shape[1]
    tm = _tile(m, 512)
    est = 2 * tm * d * 4 * 2 + 2 * tm * pd * 4 + 2 * d * d * 2 + 2 * pd * d * 2 + 2 * tm * d * 4
    rowblk = lambda i: (i, 0)
    fixed = lambda i: (0, 0)
    return pl.pallas_call(
        functools.partial(_ple_kernel, alpha=alpha),
        grid=(m // tm,),
        in_specs=[pl.BlockSpec((tm, d), rowblk), pl.BlockSpec((tm, pd), rowblk),
                  pl.BlockSpec((d, d), fixed), pl.BlockSpec((1, d), fixed), pl.BlockSpec((pd, d), fixed),
                  pl.BlockSpec((1, d), fixed), pl.BlockSpec((1, d), fixed)],
        out_specs=pl.BlockSpec((tm, d), rowblk),
        out_shape=jax.ShapeDtypeStruct((m, d), F32),
        compiler_params=pltpu.CompilerParams(
            dimension_semantics=("parallel",), vmem_limit_bytes=_vmem_limit(est)),
        name="ple_ln3",
    )(h2d, p2d, wg, bg, wp, g, b)


def kernel(x, p, w_in, conv_w, conv_b, lru_wa, lru_ba, lru_wx, lru_bx, lru_lambda, diff_lq1, diff_lk1, diff_lq2, diff_lk2, diff_subln_g, rel_bias, w_out, ln1_g, ln1_b, w_ffn_gate, w_ffn_up, w_ffn_down, ln2_g, ln2_b, w_ple_gate, b_ple_gate, w_ple_proj, ln3_g, ln3_b):
    batch, seq, d = x.shape
    depth = w_in.shape[0]
    nblk, bw = lru_wa.shape[1], lru_wa.shape[2]
    lru_w = nblk * bw
    heads = rel_bias.shape[1]
    att_qk = 2 * heads * ATT_DQK
    assert w_in.shape[2] == 2 * lru_w + 2 * att_qk + heads * ATT_DV
    assert lru_w % ATT_DV == 0 and att_qk == heads * ATT_DV
    m = batch * seq
    alpha = (2 * depth) ** 0.25

    tq = _tile(seq, 256)
    bias_tiles = _bias_tiles(rel_bias, tq)
    row = lambda v: v.reshape(1, -1)
    col = jnp.arange(2 * att_qk + heads * ATT_DV)
    qkv_scale = row(jnp.where(col < att_qk, ATT_DQK ** -0.5 * LOG2_E, 1.0).astype(F32))

    h = x.reshape(m, d)
    for i in range(depth):
        lam_init = 0.8 - 0.6 * math.exp(-0.3 * i)
        wgate = jnp.concatenate([lru_wa[i], lru_wx[i]], axis=-1).astype(BF16)
        qkv, rec = _in_proj_lru(h.reshape(batch, seq, d), w_in[i].astype(BF16), qkv_scale, conv_w[i], row(conv_b[i]),
                                wgate, row(lru_ba[i]), row(lru_bx[i]), row(lru_lambda[i]))
        lqk = jnp.stack([diff_lq1[i], diff_lk1[i], diff_lq2[i], diff_lk2[i]])
        att = _diff_attention(qkv, bias_tiles, lqk, row(diff_subln_g[i]), heads=heads, qkv_col0=0,
                              tq=tq, lam_init=lam_init)
        h = _out_proj_ln(h, rec.reshape(m, lru_w), att.reshape(m, heads * ATT_DV), w_out[i].astype(BF16),
                         row(ln1_g[i]), row(ln1_b[i]), alpha)
        h = _ffn_ln(h, w_ffn_gate[i].astype(BF16), w_ffn_up[i].astype(BF16), w_ffn_down[i].astype(BF16),
                    row(ln2_g[i]), row(ln2_b[i]), alpha)
        h = _ple_ln(h, p[i].reshape(m, -1), w_ple_gate[i].astype(BF16), row(b_ple_gate[i]),
                    w_ple_proj[i].astype(BF16), row(ln3_g[i]), row(ln3_b[i]), alpha)
    return h.reshape(batch, seq, d)
```

```python
import functools
import math

import jax
import jax.numpy as jnp
import numpy as np
from jax import lax
from jax.experimental import pallas as pl
from jax.experimental.pallas import tpu as pltpu

F32 = jnp.float32
BF16 = jnp.bfloat16

LN_EPS = 1e-5
LRU_C = 8.0
CONV_W = 4
ATT_DV = 128
ATT_DQK = ATT_DV // 2
REL_BUCKETS = 32
REL_MAX_DIST = 128
LOG2_E = math.log2(math.e)

V7X_LANES = 128
V7X_SUBLANES = 8
BF16_SUBLANES = 2 * V7X_SUBLANES
V7X_VMEM_BYTES = 64 * 1024 * 1024
V7X_VMEM_RESERVE_BYTES = 6 * 1024 * 1024


def _vmem_limit(estimate_bytes):
    return int(min(V7X_VMEM_BYTES - V7X_VMEM_RESERVE_BYTES, max(2 * estimate_bytes, 16 * 1024 * 1024)))


def _tile(dim, preferred):
    if dim % preferred == 0:
        return preferred
    return dim


def _layer_norm(y, g, b):
    mu = jnp.mean(y, axis=-1, keepdims=True)
    yc = y - mu
    var = jnp.mean(yc * yc, axis=-1, keepdims=True)
    return yc * lax.rsqrt(var + LN_EPS) * g + b


def _dot(a, b):
    return jnp.dot(a, b, preferred_element_type=F32)


def _gelu_tanh(x):
    return 0.5 * x * (1.0 + jnp.tanh(math.sqrt(2.0 / math.pi) * (x + 0.044715 * (x * x * x))))


def _sigmoid(x):
    return 0.5 * jnp.tanh(0.5 * x) + 0.5


def _in_proj_lru_kernel(x_ref, w_ref, cs_ref, cw_ref, cb_ref, wg_ref, ba_ref, bx_ref, lam_ref,
                        qkv_ref, rec_ref, xext_ref, a_ref, u_ref, h_ref, *, chunk):
    nbatch, tt, d = x_ref.shape
    rows = nbatch * tt
    lru_w = rec_ref.shape[2]
    n_qkv = qkv_ref.shape[2]
    bw = wg_ref.shape[1]
    halo = V7X_SUBLANES
    pitch = xext_ref.shape[1]
    half = V7X_SUBLANES

    @pl.when(pl.program_id(0) == 0)
    def _():
        xext_ref[:, 0:halo, :] = jnp.zeros((nbatch, halo, lru_w), F32)
        h_ref[...] = jnp.zeros(h_ref.shape, F32)

    xb = x_ref[...].reshape(rows, d).astype(BF16)
    xext_ref[:, halo:halo + tt, :] = _dot(xb, w_ref[:, 0:lru_w]).reshape(nbatch, tt, lru_w)
    yg = _dot(xb, w_ref[:, lru_w:2 * lru_w])
    for c0 in range(0, n_qkv, chunk):
        val = _dot(xb, w_ref[:, 2 * lru_w + c0:2 * lru_w + c0 + chunk]) * cs_ref[:, c0:c0 + chunk]
        qkv_ref[:, :, c0:c0 + chunk] = val.astype(BF16).reshape(nbatch, tt, chunk)

    for n in range(lru_w // bw):
        cols = slice(n * bw, (n + 1) * bw)
        xc = cb_ref[:, cols]
        for t in range(CONV_W):
            start = halo - (CONV_W - 1) + t
            xc = xc + xext_ref[:, start:start + tt, cols].reshape(rows, bw) * cw_ref[t:t + 1, cols]
        gates = _dot(xc.astype(BF16), wg_ref[n])
        gate_a = _sigmoid(gates[:, :bw] + ba_ref[:, cols])
        gate_x = _sigmoid(gates[:, bw:] + bx_ref[:, cols])
        neg_lam = -lam_ref[:, cols]
        softplus = jnp.maximum(neg_lam, 0.0) + jnp.log1p(jnp.exp(-jnp.abs(neg_lam)))
        log_a = -LRU_C * gate_a * softplus
        a = jnp.exp(log_a)
        one_minus_a2 = -jnp.tanh(log_a) * (a * a + 1.0)
        mult = jnp.where(one_minus_a2 > 0.0, one_minus_a2 * lax.rsqrt(one_minus_a2), 0.0)
        u = mult * (gate_x * xc)
        for bi in range(nbatch):
            a_ref[n, bi * pitch:bi * pitch + tt, :] = a[bi * tt:(bi + 1) * tt]
            u_ref[n, bi * pitch:bi * pitch + tt, :] = u[bi * tt:(bi + 1) * tt]

        for g in range(nbatch // half):
            h = h_ref[g * half:(g + 1) * half, cols]
            for t in range(tt):
                step = pl.ds(g * half * pitch + t, half, stride=pitch)
                h = a_ref[n, step, :] * h + u_ref[n, step, :]
                u_ref[n, step, :] = h
            h_ref[g * half:(g + 1) * half, cols] = h

        hs = jnp.concatenate([u_ref[n, bi * pitch:bi * pitch + tt, :] for bi in range(nbatch)], axis=0)
        rec_ref[:, :, cols] = (hs * _gelu_tanh(yg[:, cols])).astype(BF16).reshape(nbatch, tt, bw)

    xext_ref[:, 0:halo, :] = xext_ref[:, tt:tt + halo, :]


def _in_proj_lru(x, w_bf16, qkv_scale, conv_w, conv_b, wgate, ba, bx, lam):
    nbatch, seq, d = x.shape
    nblk, bw, _ = wgate.shape
    lru_w = nblk * bw
    n_in = w_bf16.shape[1]
    n_qkv = n_in - 2 * lru_w
    tt = _tile(seq, 32)
    rows = nbatch * tt
    pitch = tt + V7X_SUBLANES
    chunk = _tile(n_qkv, 1024)
    assert nbatch % V7X_SUBLANES == 0 and tt % BF16_SUBLANES == 0
    est = (d * n_in * 2 + 2 * rows * d * 4 + 2 * rows * (n_qkv + lru_w) * 2 + 3 * nbatch * pitch * lru_w * 4
           + rows * d * 2 + 3 * rows * lru_w * 4 + 2 * rows * chunk * 4)
    fixed = lambda t: (0, 0)
    kern = functools.partial(_in_proj_lru_kernel, chunk=chunk)
    return pl.pallas_call(
        kern,
        grid=(seq // tt,),
        in_specs=[pl.BlockSpec((nbatch, tt, d), lambda t: (0, t, 0)),
                  pl.BlockSpec((d, n_in), fixed, pipeline_mode=pl.Buffered(1)),
                  pl.BlockSpec((1, n_qkv), fixed),
                  pl.BlockSpec((CONV_W, lru_w), fixed),
                  pl.BlockSpec((1, lru_w), fixed),
                  pl.BlockSpec((nblk, bw, 2 * bw), lambda t: (0, 0, 0)),
                  pl.BlockSpec((1, lru_w), fixed),
                  pl.BlockSpec((1, lru_w), fixed),
                  pl.BlockSpec((1, lru_w), fixed)],
        out_specs=[pl.BlockSpec((nbatch, tt, n_qkv), lambda t: (0, t, 0)),
                   pl.BlockSpec((nbatch, tt, lru_w), lambda t: (0, t, 0))],
        out_shape=[jax.ShapeDtypeStruct((nbatch, seq, n_qkv), BF16),
                   jax.ShapeDtypeStruct((nbatch, seq, lru_w), BF16)],
        scratch_shapes=[pltpu.VMEM((nbatch, pitch, lru_w), F32),
                        pltpu.VMEM((nblk, nbatch * pitch, bw), F32),
                        pltpu.VMEM((nblk, nbatch * pitch, bw), F32),
                        pltpu.VMEM((nbatch, lru_w), F32)],
        compiler_params=pltpu.CompilerParams(
            dimension_semantics=("arbitrary",), vmem_limit_bytes=_vmem_limit(est)),
        name="in_proj_rg_lru",
    )(x, w_bf16, qkv_scale, conv_w, conv_b, wgate, ba, bx, lam)


def _bucket_thresholds():
    n = np.arange(0, 8 * REL_MAX_DIST)
    max_exact = REL_BUCKETS // 2
    nf = np.maximum(n, 1).astype(np.float32)
    large = max_exact + (np.log(nf / np.float32(max_exact)) / np.float32(math.log(REL_MAX_DIST / max_exact))
                         * np.float32(REL_BUCKETS - max_exact)).astype(np.int32)
    bucket = np.where(n < max_exact, n, np.minimum(large, REL_BUCKETS - 1))
    assert np.all(np.diff(bucket) >= 0) and bucket[-1] == REL_BUCKETS - 1
    return [int(np.argmax(bucket >= j)) for j in range(1, REL_BUCKETS)]


def _near_key_blocks(tq):
    last = _bucket_thresholds()[-1]
    d = 0
    while d * tq - (tq - 1) < last:
        d += 1
    return d


def _bias_kernel(table_ref, o_ref, *, thresholds):
    h = pl.program_id(0)
    nb, tk, tq = o_ref.shape[1], o_ref.shape[2], o_ref.shape[3]
    key = lax.broadcasted_iota(jnp.int32, (tk, tq), 0)
    qry = lax.broadcasted_iota(jnp.int32, (tk, tq), 1)
    far = table_ref[REL_BUCKETS - 1, h]
    for d in range(nb):
        dist = d * tk + qry - key
        val = jnp.full((tk, tq), table_ref[0, h], F32)
        for j, thr in enumerate(thresholds):
            val = jnp.where(dist >= thr, table_ref[j + 1, h], val)
        o_ref[0, d] = jnp.where(dist < 0, -jnp.inf, (val - far) * LOG2_E)


def _bias_tiles(rel_bias, tq):
    heads = rel_bias.shape[1]
    nb = _near_key_blocks(tq)
    return pl.pallas_call(
        functools.partial(_bias_kernel, thresholds=_bucket_thresholds()),
        grid=(heads,),
        in_specs=[pl.BlockSpec(memory_space=pltpu.SMEM)],
        out_specs=pl.BlockSpec((1, nb, tq, tq), lambda h: (h, 0, 0, 0)),
        out_shape=jax.ShapeDtypeStruct((heads, nb, tq, tq), F32),
        compiler_params=pltpu.CompilerParams(
            dimension_semantics=("parallel",), vmem_limit_bytes=_vmem_limit(2 * nb * tq * tq * 4)),
        name="rel_bias_tiles",
    )(rel_bias)


def _attn_kernel(q_ref, k_ref, v_ref, bias_ref, lqk_ref, g_ref, o_ref, *, tq, lam_init):
    seq = q_ref.shape[1]
    nb = bias_ref.shape[1]
    nt_dims = (((1,), (1,)), ((), ()))

    lqk = lqk_ref[...]
    lam = (jnp.exp(jnp.sum(lqk[0:1] * lqk[1:2], axis=-1, keepdims=True))
           - jnp.exp(jnp.sum(lqk[2:3] * lqk[3:4], axis=-1, keepdims=True)) + lam_init)
    ones_rows = (lax.broadcasted_iota(jnp.int32, (BF16_SUBLANES, seq), 0) == 0).astype(F32)
    v_aug_t = jnp.concatenate([v_ref[0].astype(F32).T, ones_rows], axis=0).astype(BF16)
    lane = lax.broadcasted_iota(jnp.int32, (tq, ATT_DV), 1)

    def scores(i):
        q = q_ref[0, i * tq:(i + 1) * tq, :]
        k = k_ref[0, 0:(i + 1) * tq, :]
        n_far = max(i + 1 - nb, 0)
        out = []
        for c in range(2):
            q_c = jnp.where((lane >= ATT_DQK) if c else (lane < ATT_DQK), q, jnp.zeros_like(q))
            s_t = lax.dot_general(k, q_c, nt_dims, preferred_element_type=F32)
            parts = [s_t[0:n_far * tq]] if n_far else []
            for j in range(n_far, i + 1):
                parts.append(s_t[j * tq:(j + 1) * tq] + bias_ref[0, i - j])
            m = functools.reduce(jnp.maximum, [jnp.max(part, axis=0, keepdims=True) for part in parts])
            out.append((parts, m))
        return out

    n_q = seq // tq
    pending = scores(0)
    for i in range(n_q):
        current, pending = pending, (scores(i + 1) if i + 1 < n_q else None)
        n_keys = (i + 1) * tq
        probs = [jnp.concatenate([jnp.exp2(part - m).astype(BF16) for part in parts], axis=0) for parts, m in current]
        pv = _dot(v_aug_t[:, 0:n_keys], jnp.concatenate(probs, axis=1))
        num1, num2 = pv[0:ATT_DV, 0:tq], pv[0:ATT_DV, tq:]
        sum1, sum2 = pv[ATT_DV:ATT_DV + 1, 0:tq], pv[ATT_DV:ATT_DV + 1, tq:]
        out_t = num1 * (1.0 / sum1) - num2 * (lam / sum2)
        rms = lax.rsqrt(jnp.mean(out_t * out_t, axis=0, keepdims=True) + LN_EPS)
        o_ref[0, i * tq:(i + 1) * tq, :] = ((out_t * rms).T * g_ref[...] * (1.0 - lam_init)).astype(o_ref.dtype)


def _diff_attention(proj3d, bias_tiles, lqk, subln_g, *, heads, qkv_col0, tq, lam_init):
    b, s, _ = proj3d.shape
    nb = bias_tiles.shape[1]
    q_blk0 = qkv_col0 // ATT_DV
    k_blk0 = q_blk0 + heads
    v_blk0 = k_blk0 + heads
    live_q_blocks = 4
    est = 2 * 4 * s * ATT_DV * 2 + 2 * nb * tq * tq * 4 + live_q_blocks * 2 * 2 * s * tq * 4
    kern = functools.partial(_attn_kernel, tq=tq, lam_init=lam_init)
    return pl.pallas_call(
        kern,
        grid=(b, heads),
        in_specs=[pl.BlockSpec((1, s, ATT_DV), lambda bi, h: (bi, 0, q_blk0 + h)),
                  pl.BlockSpec((1, s, ATT_DV), lambda bi, h: (bi, 0, k_blk0 + h)),
                  pl.BlockSpec((1, s, ATT_DV), lambda bi, h: (bi, 0, v_blk0 + h)),
                  pl.BlockSpec((1, nb, tq, tq), lambda bi, h: (h, 0, 0, 0)),
                  pl.BlockSpec(lqk.shape, lambda bi, h: (0, 0)),
                  pl.BlockSpec((1, ATT_DV), lambda bi, h: (0, 0))],
        out_specs=pl.BlockSpec((1, s, ATT_DV), lambda bi, h: (bi, 0, h)),
        out_shape=jax.ShapeDtypeStruct((b, s, heads * ATT_DV), BF16),
        compiler_params=pltpu.CompilerParams(
            dimension_semantics=("parallel", "parallel"), vmem_limit_bytes=_vmem_limit(est)),
        name="diff_attention",
    )(proj3d, proj3d, proj3d, bias_tiles, lqk, subln_g)


def _out_proj_kernel(x_ref, rec_ref, att_ref, w_ref, g_ref, b_ref, o_ref, ob_ref, *, alpha):
    w1 = rec_ref.shape[1]
    m = _dot(rec_ref[...], w_ref[0:w1, :]) + _dot(att_ref[...], w_ref[w1:, :])
    h = _layer_norm(alpha * x_ref[...] + m, g_ref[...], b_ref[...])
    o_ref[...] = h
    ob_ref[...] = h.astype(BF16)


def _out_proj_ln(x2d, rec2d, att2d, w_bf16, g, b, alpha):
    m, d = x2d.shape
    w1, w2 = rec2d.shape[1], att2d.shape[1]
    tm = _tile(m, 512)
    est = 2 * tm * d * 4 * 2 + 2 * tm * (w1 + w2) * 2 + 2 * (w1 + w2) * d * 2 + 2 * tm * d * 4 + 2 * tm * d * 2
    rowblk = lambda i: (i, 0)
    fixed = lambda i: (0, 0)
    return pl.pallas_call(
        functools.partial(_out_proj_kernel, alpha=alpha),
        grid=(m // tm,),
        in_specs=[pl.BlockSpec((tm, d), rowblk), pl.BlockSpec((tm, w1), rowblk), pl.BlockSpec((tm, w2), rowblk),
                  pl.BlockSpec((w1 + w2, d), fixed), pl.BlockSpec((1, d), fixed), pl.BlockSpec((1, d), fixed)],
        out_specs=[pl.BlockSpec((tm, d), rowblk), pl.BlockSpec((tm, d), rowblk)],
        out_shape=[jax.ShapeDtypeStruct((m, d), F32), jax.ShapeDtypeStruct((m, d), BF16)],
        compiler_params=pltpu.CompilerParams(
            dimension_semantics=("parallel",), vmem_limit_bytes=_vmem_limit(est)),
        name="out_proj_ln1",
    )(x2d, rec2d, att2d, w_bf16, g, b)


def _ffn_kernel(hb_ref, wg_ref, wu_ref, wd_ref, o_ref):
    @pl.when(pl.program_id(1) == 0)
    def _():
        o_ref[...] = jnp.zeros(o_ref.shape, F32)

    hb = hb_ref[...]
    gate = _dot(hb, wg_ref[...])
    up = _dot(hb, wu_ref[...])
    act = (gate * jax.nn.sigmoid(gate)) * up
    o_ref[...] += _dot(act.astype(BF16), wd_ref[...])


def _ffn(hb2d, wg, wu, wd):
    m, d = hb2d.shape
    f = wg.shape[1]
    tm, tf = _tile(m, 1024), _tile(f, 512)
    est = 2 * tm * d * 2 + 2 * 3 * d * tf * 2 + 2 * tm * d * 4 + 3 * tm * tf * 4 + tm * d * 4
    rowblk = lambda i, j: (i, 0)
    return pl.pallas_call(
        _ffn_kernel,
        grid=(m // tm, f // tf),
        in_specs=[pl.BlockSpec((tm, d), rowblk),
                  pl.BlockSpec((d, tf), lambda i, j: (0, j)),
                  pl.BlockSpec((d, tf), lambda i, j: (0, j)),
                  pl.BlockSpec((tf, d), lambda i, j: (j, 0))],
        out_specs=pl.BlockSpec((tm, d), rowblk),
        out_shape=jax.ShapeDtypeStruct((m, d), F32),
        compiler_params=pltpu.CompilerParams(
            dimension_semantics=("parallel", "arbitrary"), vmem_limit_bytes=_vmem_limit(est)),
        name="swiglu",
    )(hb2d, wg, wu, wd)


def _ple_kernel(h_ref, f_ref, p_ref, g2_ref, b2_ref, wg_ref, bg_ref, wp_ref, g3_ref, b3_ref, o_ref, *, alpha):
    h = _layer_norm(alpha * h_ref[...] + f_ref[...], g2_ref[...], b2_ref[...])
    gate = jax.nn.sigmoid(_dot(h.astype(BF16), wg_ref[...]) + bg_ref[...])
    e = _dot(p_ref[...].astype(BF16), wp_ref[...])
    o_ref[...] = _layer_norm(alpha * h + gate * e, g3_ref[...], b3_ref[...])


def _ln2_ple_ln3(h2d, f2d, p2d, g2, b2, wg, bg, wp, g3, b3, alpha):
    m, d = h2d.shape
    pd = p2d.shape[1]
    tm = _tile(m, 512)
    est = 3 * 2 * tm * d * 4 + 2 * tm * pd * 4 + 2 * d * d * 2 + 2 * pd * d * 2 + 3 * tm * d * 4
    rowblk = lambda i: (i, 0)
    fixed = lambda i: (0, 0)
    vec = pl.BlockSpec((1, d), fixed)
    return pl.pallas_call(
        functools.partial(_ple_kernel, alpha=alpha),
        grid=(m // tm,),
        in_specs=[pl.BlockSpec((tm, d), rowblk), pl.BlockSpec((tm, d), rowblk), pl.BlockSpec((tm, pd), rowblk),
                  vec, vec, pl.BlockSpec((d, d), fixed), vec, pl.BlockSpec((pd, d), fixed), vec, vec],
        out_specs=pl.BlockSpec((tm, d), rowblk),
        out_shape=jax.ShapeDtypeStruct((m, d), F32),
        compiler_params=pltpu.CompilerParams(
            dimension_semantics=("parallel",), vmem_limit_bytes=_vmem_limit(est)),
        name="ln2_ple_ln3",
    )(h2d, f2d, p2d, g2, b2, wg, bg, wp, g3, b3)


def kernel(x, p, w_in, conv_w, conv_b, lru_wa, lru_ba, lru_wx, lru_bx, lru_lambda, diff_lq1, diff_lk1, diff_lq2, diff_lk2, diff_subln_g, rel_bias, w_out, ln1_g, ln1_b, w_ffn_gate, w_ffn_up, w_ffn_down, ln2_g, ln2_b, w_ple_gate, b_ple_gate, w_ple_proj, ln3_g, ln3_b):
    batch, seq, d = x.shape
    depth = w_in.shape[0]
    nblk, bw = lru_wa.shape[1], lru_wa.shape[2]
    lru_w = nblk * bw
    heads = rel_bias.shape[1]
    att_qk = 2 * heads * ATT_DQK
    assert w_in.shape[2] == 2 * lru_w + 2 * att_qk + heads * ATT_DV
    assert lru_w % ATT_DV == 0 and att_qk == heads * ATT_DV
    m = batch * seq
    alpha = (2 * depth) ** 0.25

    tq = _tile(seq, 256)
    bias_tiles = _bias_tiles(rel_bias, tq)
    row = lambda v: v.reshape(1, -1)
    col = jnp.arange(2 * att_qk + heads * ATT_DV)
    qkv_scale = row(jnp.where(col < att_qk, ATT_DQK ** -0.5 * LOG2_E, 1.0).astype(F32))

    h = x.reshape(m, d)
    for i in range(depth):
        lam_init = 0.8 - 0.6 * math.exp(-0.3 * i)
        wgate = jnp.concatenate([lru_wa[i], lru_wx[i]], axis=-1).astype(BF16)
        qkv, rec = _in_proj_lru(h.reshape(batch, seq, d), w_in[i].astype(BF16), qkv_scale, conv_w[i], row(conv_b[i]),
                                wgate, row(lru_ba[i]), row(lru_bx[i]), row(lru_lambda[i]))
        lqk = jnp.stack([diff_lq1[i], diff_lk1[i], diff_lq2[i], diff_lk2[i]])
        att = _diff_attention(qkv, bias_tiles, lqk, row(diff_subln_g[i]), heads=heads, qkv_col0=0,
                              tq=tq, lam_init=lam_init)
        h, hb = _out_proj_ln(h, rec.reshape(m, lru_w), att.reshape(m, heads * ATT_DV), w_out[i].astype(BF16),
                             row(ln1_g[i]), row(ln1_b[i]), alpha)
        f = _ffn(hb, w_ffn_gate[i].astype(BF16), w_ffn_up[i].astype(BF16), w_ffn_down[i].astype(BF16))
        h = _ln2_ple_ln3(h, f, p[i].reshape(m, -1), row(ln2_g[i]), row(ln2_b[i]), w_ple_gate[i].astype(BF16),
                         row(b_ple_gate[i]), w_ple_proj[i].astype(BF16), row(ln3_g[i]), row(ln3_b[i]), alpha)
    return h.reshape(batch, seq, d)
```

```python
import functools
import math

import jax
import jax.numpy as jnp
import numpy as np
from jax import lax
from jax.experimental import pallas as pl
from jax.experimental.pallas import tpu as pltpu

F32 = jnp.float32
BF16 = jnp.bfloat16

LN_EPS = 1e-5
LRU_C = 8.0
LRU_BATCH_GROUP = 4
CONV_W = 4
ATT_DV = 128
ATT_DQK = ATT_DV // 2
REL_BUCKETS = 32
REL_MAX_DIST = 128
ATT_PIPELINE_DEPTH = 2
LOG2_E = math.log2(math.e)

V7X_LANES = 128
V7X_SUBLANES = 8
BF16_SUBLANES = 2 * V7X_SUBLANES
V7X_VMEM_BYTES = 64 * 1024 * 1024
V7X_VMEM_RESERVE_BYTES = 6 * 1024 * 1024


def _vmem_limit(estimate_bytes):
    return int(min(V7X_VMEM_BYTES - V7X_VMEM_RESERVE_BYTES, max(2 * estimate_bytes, 16 * 1024 * 1024)))


def _tile(dim, preferred):
    if dim % preferred == 0:
        return preferred
    return dim


def _layer_norm(y, g, b):
    mu = jnp.mean(y, axis=-1, keepdims=True)
    yc = y - mu
    var = jnp.mean(yc * yc, axis=-1, keepdims=True)
    return yc * lax.rsqrt(var + LN_EPS) * g + b


def _dot(a, b):
    return jnp.dot(a, b, preferred_element_type=F32)


def _gelu_tanh(x):
    return 0.5 * x * (1.0 + jnp.tanh(math.sqrt(2.0 / math.pi) * (x + 0.044715 * (x * x * x))))


def _sigmoid(x):
    return 0.5 * jnp.tanh(0.5 * x) + 0.5


def _in_proj_lru_kernel(x_ref, w_ref, cs_ref, cw_ref, cb_ref, wg_ref, ba_ref, bx_ref, lam_ref,
                        qkv_ref, rec_ref, xext_ref, a_ref, u_ref, h_ref, *, chunk):
    nbatch, tt, d = x_ref.shape
    rows = nbatch * tt
    lru_w = rec_ref.shape[2]
    n_qkv = qkv_ref.shape[2]
    bw = wg_ref.shape[1]
    halo = V7X_SUBLANES
    pitch = xext_ref.shape[1]
    half = V7X_SUBLANES

    @pl.when(pl.program_id(0) == 0)
    def _():
        xext_ref[:, 0:halo, :] = jnp.zeros((nbatch, halo, lru_w), F32)
        h_ref[...] = jnp.zeros(h_ref.shape, F32)

    xb = x_ref[...].reshape(rows, d).astype(BF16)
    xext_ref[:, halo:halo + tt, :] = _dot(xb, w_ref[:, 0:lru_w]).reshape(nbatch, tt, lru_w)
    yg = _dot(xb, w_ref[:, lru_w:2 * lru_w])
    def qkv_chunk(c0):
        val = _dot(xb, w_ref[:, 2 * lru_w + c0:2 * lru_w + c0 + chunk]) * cs_ref[:, c0:c0 + chunk]
        qkv_ref[:, :, c0:c0 + chunk] = val.astype(BF16).reshape(nbatch, tt, chunk)

    qkv_starts = list(range(0, n_qkv, chunk))
    gb = LRU_BATCH_GROUP
    for n in range(lru_w // bw):
        cols = slice(n * bw, (n + 1) * bw)
        neg_lam = -lam_ref[:, cols]
        softplus = jnp.maximum(neg_lam, 0.0) + jnp.log1p(jnp.exp(-jnp.abs(neg_lam)))
        for b0 in range(0, nbatch, gb):
            sub = gb * tt
            xc = cb_ref[:, cols]
            for t in range(CONV_W):
                start = halo - (CONV_W - 1) + t
                xc = xc + xext_ref[b0:b0 + gb, start:start + tt, cols].reshape(sub, bw) * cw_ref[t:t + 1, cols]
            gates = _dot(xc.astype(BF16), wg_ref[n])
            gate_a = _sigmoid(gates[:, :bw] + ba_ref[:, cols])
            gate_x = _sigmoid(gates[:, bw:] + bx_ref[:, cols])
            log_a = -LRU_C * gate_a * softplus
            a = jnp.exp(log_a)
            one_minus_a2 = -jnp.tanh(log_a) * (a * a + 1.0)
            mult = jnp.where(one_minus_a2 > 0.0, one_minus_a2 * lax.rsqrt(one_minus_a2), 0.0)
            u = mult * (gate_x * xc)
            for bi in range(gb):
                a_ref[n, (b0 + bi) * pitch:(b0 + bi) * pitch + tt, :] = a[bi * tt:(bi + 1) * tt]
                u_ref[n, (b0 + bi) * pitch:(b0 + bi) * pitch + tt, :] = u[bi * tt:(bi + 1) * tt]

        for g in range(nbatch // half):
            h = h_ref[g * half:(g + 1) * half, cols]
            for t in range(tt):
                step = pl.ds(g * half * pitch + t, half, stride=pitch)
                h = a_ref[n, step, :] * h + u_ref[n, step, :]
                u_ref[n, step, :] = h
            h_ref[g * half:(g + 1) * half, cols] = h

        for b0 in range(0, nbatch, gb):
            hs = jnp.concatenate([u_ref[n, bi * pitch:bi * pitch + tt, :] for bi in range(b0, b0 + gb)], axis=0)
            gelu = _gelu_tanh(yg[b0 * tt:(b0 + gb) * tt, cols])
            rec_ref[b0:b0 + gb, :, cols] = (hs * gelu).astype(BF16).reshape(gb, tt, bw)
        for c0 in qkv_starts[n::lru_w // bw]:
            qkv_chunk(c0)

    xext_ref[:, 0:halo, :] = xext_ref[:, tt:tt + halo, :]


def _in_proj_lru(x, w_bf16, qkv_scale, conv_w, conv_b, wgate, ba, bx, lam):
    nbatch, seq, d = x.shape
    nblk, bw, _ = wgate.shape
    lru_w = nblk * bw
    n_in = w_bf16.shape[1]
    n_qkv = n_in - 2 * lru_w
    tt = _tile(seq, 32)
    rows = nbatch * tt
    pitch = tt + V7X_SUBLANES
    chunk = _tile(n_qkv, 512)
    assert nbatch % V7X_SUBLANES == 0 and tt % BF16_SUBLANES == 0
    est = (d * n_in * 2 + 2 * rows * d * 4 + 2 * rows * (n_qkv + lru_w) * 2 + 3 * nbatch * pitch * lru_w * 4
           + rows * d * 2 + 3 * rows * lru_w * 4 + 2 * rows * chunk * 4)
    fixed = lambda t: (0, 0)
    kern = functools.partial(_in_proj_lru_kernel, chunk=chunk)
    return pl.pallas_call(
        kern,
        grid=(seq // tt,),
        in_specs=[pl.BlockSpec((nbatch, tt, d), lambda t: (0, t, 0)),
                  pl.BlockSpec((d, n_in), fixed, pipeline_mode=pl.Buffered(1)),
                  pl.BlockSpec((1, n_qkv), fixed),
                  pl.BlockSpec((CONV_W, lru_w), fixed),
                  pl.BlockSpec((1, lru_w), fixed),
                  pl.BlockSpec((nblk, bw, 2 * bw), lambda t: (0, 0, 0)),
                  pl.BlockSpec((1, lru_w), fixed),
                  pl.BlockSpec((1, lru_w), fixed),
                  pl.BlockSpec((1, lru_w), fixed)],
        out_specs=[pl.BlockSpec((nbatch, tt, n_qkv), lambda t: (0, t, 0)),
                   pl.BlockSpec((nbatch, tt, lru_w), lambda t: (0, t, 0))],
        out_shape=[jax.ShapeDtypeStruct((nbatch, seq, n_qkv), BF16),
                   jax.ShapeDtypeStruct((nbatch, seq, lru_w), BF16)],
        scratch_shapes=[pltpu.VMEM((nbatch, pitch, lru_w), F32),
                        pltpu.VMEM((nblk, nbatch * pitch, bw), F32),
                        pltpu.VMEM((nblk, nbatch * pitch, bw), F32),
                        pltpu.VMEM((nbatch, lru_w), F32)],
        compiler_params=pltpu.CompilerParams(
            dimension_semantics=("arbitrary",), vmem_limit_bytes=_vmem_limit(est)),
        name="in_proj_rg_lru",
    )(x, w_bf16, qkv_scale, conv_w, conv_b, wgate, ba, bx, lam)


def _bucket_thresholds():
    n = np.arange(0, 8 * REL_MAX_DIST)
    max_exact = REL_BUCKETS // 2
    nf = np.maximum(n, 1).astype(np.float32)
    large = max_exact + (np.log(nf / np.float32(max_exact)) / np.float32(math.log(REL_MAX_DIST / max_exact))
                         * np.float32(REL_BUCKETS - max_exact)).astype(np.int32)
    bucket = np.where(n < max_exact, n, np.minimum(large, REL_BUCKETS - 1))
    assert np.all(np.diff(bucket) >= 0) and bucket[-1] == REL_BUCKETS - 1
    return [int(np.argmax(bucket >= j)) for j in range(1, REL_BUCKETS)]


def _near_key_blocks(tq):
    last = _bucket_thresholds()[-1]
    d = 0
    while d * tq - (tq - 1) < last:
        d += 1
    return d


def _bias_kernel(table_ref, o_ref, *, thresholds):
    h = pl.program_id(0)
    nb, tk, tq = o_ref.shape[1], o_ref.shape[2], o_ref.shape[3]
    key = lax.broadcasted_iota(jnp.int32, (tk, tq), 0)
    qry = lax.broadcasted_iota(jnp.int32, (tk, tq), 1)
    far = table_ref[REL_BUCKETS - 1, h]
    for d in range(nb):
        dist = d * tk + qry - key
        val = jnp.full((tk, tq), table_ref[0, h], F32)
        for j, thr in enumerate(thresholds):
            val = jnp.where(dist >= thr, table_ref[j + 1, h], val)
        o_ref[0, d] = jnp.where(dist < 0, -jnp.inf, (val - far) * LOG2_E)


def _bias_tiles(rel_bias, tq):
    heads = rel_bias.shape[1]
    nb = _near_key_blocks(tq)
    return pl.pallas_call(
        functools.partial(_bias_kernel, thresholds=_bucket_thresholds()),
        grid=(heads,),
        in_specs=[pl.BlockSpec(memory_space=pltpu.SMEM)],
        out_specs=pl.BlockSpec((1, nb, tq, tq), lambda h: (h, 0, 0, 0)),
        out_shape=jax.ShapeDtypeStruct((heads, nb, tq, tq), F32),
        compiler_params=pltpu.CompilerParams(
            dimension_semantics=("parallel",), vmem_limit_bytes=_vmem_limit(2 * nb * tq * tq * 4)),
        name="rel_bias_tiles",
    )(rel_bias)


def _attn_kernel(q_ref, k_ref, v_ref, bias_ref, lqk_ref, g_ref, o_ref, *, tq, lam_init):
    seq = q_ref.shape[1]
    nb = bias_ref.shape[1]
    nt_dims = (((1,), (1,)), ((), ()))

    lqk = lqk_ref[...]
    lam = (jnp.exp(jnp.sum(lqk[0:1] * lqk[1:2], axis=-1, keepdims=True))
           - jnp.exp(jnp.sum(lqk[2:3] * lqk[3:4], axis=-1, keepdims=True)) + lam_init)
    ones_rows = (lax.broadcasted_iota(jnp.int32, (BF16_SUBLANES, seq), 0) == 0).astype(F32)
    v_aug_t = jnp.concatenate([v_ref[0].astype(F32).T, ones_rows], axis=0).astype(BF16)
    lane = lax.broadcasted_iota(jnp.int32, (tq, ATT_DV), 1)

    def scores(i):
        q = q_ref[0, i * tq:(i + 1) * tq, :]
        k = k_ref[0, 0:(i + 1) * tq, :]
        n_far = max(i + 1 - nb, 0)
        out = []
        for c in range(2):
            q_c = jnp.where((lane >= ATT_DQK) if c else (lane < ATT_DQK), q, jnp.zeros_like(q))
            s_t = lax.dot_general(k, q_c, nt_dims, preferred_element_type=F32)
            parts = [s_t[0:n_far * tq]] if n_far else []
            for j in range(n_far, i + 1):
                parts.append(s_t[j * tq:(j + 1) * tq] + bias_ref[0, i - j])
            m = functools.reduce(jnp.maximum, [jnp.max(part, axis=0, keepdims=True) for part in parts])
            out.append((parts, m))
        return out

    n_q = seq // tq
    queue = [scores(i) for i in range(min(ATT_PIPELINE_DEPTH, n_q))]
    for i in range(n_q):
        current = queue.pop(0)
        if i + ATT_PIPELINE_DEPTH < n_q:
            queue.append(scores(i + ATT_PIPELINE_DEPTH))
        n_keys = (i + 1) * tq
        probs = [jnp.concatenate([jnp.exp2(part - m).astype(BF16) for part in parts], axis=0) for parts, m in current]
        pv = _dot(v_aug_t[:, 0:n_keys], jnp.concatenate(probs, axis=1))
        num1, num2 = pv[0:ATT_DV, 0:tq], pv[0:ATT_DV, tq:]
        sum1, sum2 = pv[ATT_DV:ATT_DV + 1, 0:tq], pv[ATT_DV:ATT_DV + 1, tq:]
        out_t = num1 * (1.0 / sum1) - num2 * (lam / sum2)
        rms = lax.rsqrt(jnp.mean(out_t * out_t, axis=0, keepdims=True) + LN_EPS)
        o_ref[0, i * tq:(i + 1) * tq, :] = ((out_t * rms).T * g_ref[...] * (1.0 - lam_init)).astype(o_ref.dtype)


def _diff_attention(proj3d, bias_tiles, lqk, subln_g, *, heads, qkv_col0, tq, lam_init):
    b, s, _ = proj3d.shape
    nb = bias_tiles.shape[1]
    q_blk0 = qkv_col0 // ATT_DV
    k_blk0 = q_blk0 + heads
    v_blk0 = k_blk0 + heads
    live_q_blocks = 4
    est = 2 * 4 * s * ATT_DV * 2 + 2 * nb * tq * tq * 4 + live_q_blocks * 2 * 2 * s * tq * 4
    kern = functools.partial(_attn_kernel, tq=tq, lam_init=lam_init)
    return pl.pallas_call(
        kern,
        grid=(b, heads),
        in_specs=[pl.BlockSpec((1, s, ATT_DV), lambda bi, h: (bi, 0, q_blk0 + h)),
                  pl.BlockSpec((1, s, ATT_DV), lambda bi, h: (bi, 0, k_blk0 + h)),
                  pl.BlockSpec((1, s, ATT_DV), lambda bi, h: (bi, 0, v_blk0 + h)),
                  pl.BlockSpec((1, nb, tq, tq), lambda bi, h: (h, 0, 0, 0)),
                  pl.BlockSpec(lqk.shape, lambda bi, h: (0, 0)),
                  pl.BlockSpec((1, ATT_DV), lambda bi, h: (0, 0))],
        out_specs=pl.BlockSpec((1, s, ATT_DV), lambda bi, h: (bi, 0, h)),
        out_shape=jax.ShapeDtypeStruct((b, s, heads * ATT_DV), BF16),
        compiler_params=pltpu.CompilerParams(
            dimension_semantics=("parallel", "parallel"), vmem_limit_bytes=_vmem_limit(est)),
        name="diff_attention",
    )(proj3d, proj3d, proj3d, bias_tiles, lqk, subln_g)


def _out_proj_kernel(x_ref, rec_ref, att_ref, w_ref, g_ref, b_ref, o_ref, ob_ref, *, alpha):
    w1 = rec_ref.shape[1]
    m = _dot(rec_ref[...], w_ref[0:w1, :]) + _dot(att_ref[...], w_ref[w1:, :])
    h = _layer_norm(alpha * x_ref[...] + m, g_ref[...], b_ref[...])
    o_ref[...] = h
    ob_ref[...] = h.astype(BF16)


def _out_proj_ln(x2d, rec2d, att2d, w_bf16, g, b, alpha):
    m, d = x2d.shape
    w1, w2 = rec2d.shape[1], att2d.shape[1]
    tm = _tile(m, 512)
    est = 2 * tm * d * 4 * 2 + 2 * tm * (w1 + w2) * 2 + 2 * (w1 + w2) * d * 2 + 2 * tm * d * 4 + 2 * tm * d * 2
    rowblk = lambda i: (i, 0)
    fixed = lambda i: (0, 0)
    return pl.pallas_call(
        functools.partial(_out_proj_kernel, alpha=alpha),
        grid=(m // tm,),
        in_specs=[pl.BlockSpec((tm, d), rowblk), pl.BlockSpec((tm, w1), rowblk), pl.BlockSpec((tm, w2), rowblk),
                  pl.BlockSpec((w1 + w2, d), fixed), pl.BlockSpec((1, d), fixed), pl.BlockSpec((1, d), fixed)],
        out_specs=[pl.BlockSpec((tm, d), rowblk), pl.BlockSpec((tm, d), rowblk)],
        out_shape=[jax.ShapeDtypeStruct((m, d), F32), jax.ShapeDtypeStruct((m, d), BF16)],
        compiler_params=pltpu.CompilerParams(
            dimension_semantics=("parallel",), vmem_limit_bytes=_vmem_limit(est)),
        name="out_proj_ln1",
    )(x2d, rec2d, att2d, w_bf16, g, b)


def _ffn_kernel(hb_ref, wg_ref, wu_ref, wd_ref, o_ref):
    @pl.when(pl.program_id(1) == 0)
    def _():
        o_ref[...] = jnp.zeros(o_ref.shape, F32)

    hb = hb_ref[...]
    gate = _dot(hb, wg_ref[...])
    up = _dot(hb, wu_ref[...])
    act = (gate * jax.nn.sigmoid(gate)) * up
    o_ref[...] += _dot(act.astype(BF16), wd_ref[...])


def _ffn(hb2d, wg, wu, wd):
    m, d = hb2d.shape
    f = wg.shape[1]
    tm, tf = _tile(m, 1024), _tile(f, 512)
    est = 2 * tm * d * 2 + 2 * 3 * d * tf * 2 + 2 * tm * d * 4 + 3 * tm * tf * 4 + tm * d * 4
    rowblk = lambda i, j: (i, 0)
    return pl.pallas_call(
        _ffn_kernel,
        grid=(m // tm, f // tf),
        in_specs=[pl.BlockSpec((tm, d), rowblk),
                  pl.BlockSpec((d, tf), lambda i, j: (0, j)),
                  pl.BlockSpec((d, tf), lambda i, j: (0, j)),
                  pl.BlockSpec((tf, d), lambda i, j: (j, 0))],
        out_specs=pl.BlockSpec((tm, d), rowblk),
        out_shape=jax.ShapeDtypeStruct((m, d), F32),
        compiler_params=pltpu.CompilerParams(
            dimension_semantics=("parallel", "arbitrary"), vmem_limit_bytes=_vmem_limit(est)),
        name="swiglu",
    )(hb2d, wg, wu, wd)


def _ple_kernel(h_ref, f_ref, p_ref, g2_ref, b2_ref, wg_ref, bg_ref, wp_ref, g3_ref, b3_ref, o_ref, *, alpha):
    h = _layer_norm(alpha * h_ref[...] + f_ref[...], g2_ref[...], b2_ref[...])
    gate = jax.nn.sigmoid(_dot(h.astype(BF16), wg_ref[...]) + bg_ref[...])
    e = _dot(p_ref[...].astype(BF16), wp_ref[...])
    o_ref[...] = _layer_norm(alpha * h + gate * e, g3_ref[...], b3_ref[...])


def _ln2_ple_ln3(h2d, f2d, p2d, g2, b2, wg, bg, wp, g3, b3, alpha):
    m, d = h2d.shape
    pd = p2d.shape[1]
    tm = _tile(m, 512)
    est = 3 * 2 * tm * d * 4 + 2 * tm * pd * 4 + 2 * d * d * 2 + 2 * pd * d * 2 + 3 * tm * d * 4
    rowblk = lambda i: (i, 0)
    fixed = lambda i: (0, 0)
    vec = pl.BlockSpec((1, d), fixed)
    return pl.pallas_call(
        functools.partial(_ple_kernel, alpha=alpha),
        grid=(m // tm,),
        in_specs=[pl.BlockSpec((tm, d), rowblk), pl.BlockSpec((tm, d), rowblk), pl.BlockSpec((tm, pd), rowblk),
                  vec, vec, pl.BlockSpec((d, d), fixed), vec, pl.BlockSpec((pd, d), fixed), vec, vec],
        out_specs=pl.BlockSpec((tm, d), rowblk),
        out_shape=jax.ShapeDtypeStruct((m, d), F32),
        compiler_params=pltpu.CompilerParams(
            dimension_semantics=("parallel",), vmem_limit_bytes=_vmem_limit(est)),
        name="ln2_ple_ln3",
    )(h2d, f2d, p2d, g2, b2, wg, bg, wp, g3, b3)


def kernel(x, p, w_in, conv_w, conv_b, lru_wa, lru_ba, lru_wx, lru_bx, lru_lambda, diff_lq1, diff_lk1, diff_lq2, diff_lk2, diff_subln_g, rel_bias, w_out, ln1_g, ln1_b, w_ffn_gate, w_ffn_up, w_ffn_down, ln2_g, ln2_b, w_ple_gate, b_ple_gate, w_ple_proj, ln3_g, ln3_b):
    batch, seq, d = x.shape
    depth = w_in.shape[0]
    nblk, bw = lru_wa.shape[1], lru_wa.shape[2]
    lru_w = nblk * bw
    heads = rel_bias.shape[1]
    att_qk = 2 * heads * ATT_DQK
    assert w_in.shape[2] == 2 * lru_w + 2 * att_qk + heads * ATT_DV
    assert lru_w % ATT_DV == 0 and att_qk == heads * ATT_DV
    m = batch * seq
    alpha = (2 * depth) ** 0.25

    tq = _tile(seq, 256)
    bias_tiles = _bias_tiles(rel_bias, tq)
    row = lambda v: v.reshape(1, -1)
    col = jnp.arange(2 * att_qk + heads * ATT_DV)
    qkv_scale = row(jnp.where(col < att_qk, ATT_DQK ** -0.5 * LOG2_E, 1.0).astype(F32))

    h = x.reshape(m, d)
    for i in range(depth):
        lam_init = 0.8 - 0.6 * math.exp(-0.3 * i)
        wgate = jnp.concatenate([lru_wa[i], lru_wx[i]], axis=-1).astype(BF16)
        qkv, rec = _in_proj_lru(h.reshape(batch, seq, d), w_in[i].astype(BF16), qkv_scale, conv_w[i], row(conv_b[i]),
                                wgate, row(lru_ba[i]), row(lru_bx[i]), row(lru_lambda[i]))
        lqk = jnp.stack([diff_lq1[i], diff_lk1[i], diff_lq2[i], diff_lk2[i]])
        att = _diff_attention(qkv, bias_tiles, lqk, row(diff_subln_g[i]), heads=heads, qkv_col0=0,
                              tq=tq, lam_init=lam_init)
        h, hb = _out_proj_ln(h, rec.reshape(m, lru_w), att.reshape(m, heads * ATT_DV), w_out[i].astype(BF16),
                             row(ln1_g[i]), row(ln1_b[i]), alpha)
        f = _ffn(hb, w_ffn_gate[i].astype(BF16), w_ffn_up[i].astype(BF16), w_ffn_down[i].astype(BF16))
        h = _ln2_ple_ln3(h, f, p[i].reshape(m, -1), row(ln2_g[i]), row(ln2_b[i]), w_ple_gate[i].astype(BF16),
                         row(b_ple_gate[i]), w_ple_proj[i].astype(BF16), row(ln3_g[i]), row(ln3_b[i]), alpha)
    return h.reshape(batch, seq, d)
```

```python
import functools
import math

import jax
import jax.numpy as jnp
import numpy as np
from jax import lax
from jax.experimental import pallas as pl
from jax.experimental.pallas import tpu as pltpu

F32 = jnp.float32
BF16 = jnp.bfloat16

LN_EPS = 1e-5
LRU_C = 8.0
CONV_W = 4
ATT_DV = 128
ATT_DQK = ATT_DV // 2
REL_BUCKETS = 32
REL_MAX_DIST = 128
ATT_PIPELINE_DEPTH = 2
LOG2_E = math.log2(math.e)

V7X_LANES = 128
V7X_SUBLANES = 8
BF16_SUBLANES = 2 * V7X_SUBLANES
V7X_VMEM_BYTES = 64 * 1024 * 1024
V7X_VMEM_RESERVE_BYTES = 6 * 1024 * 1024


def _vmem_limit(estimate_bytes):
    return int(min(V7X_VMEM_BYTES - V7X_VMEM_RESERVE_BYTES, max(2 * estimate_bytes, 16 * 1024 * 1024)))


def _tile(dim, preferred):
    if dim % preferred == 0:
        return preferred
    return dim


def _layer_norm(y, g, b):
    mu = jnp.mean(y, axis=-1, keepdims=True)
    yc = y - mu
    var = jnp.mean(yc * yc, axis=-1, keepdims=True)
    return yc * lax.rsqrt(var + LN_EPS) * g + b


def _dot(a, b):
    return jnp.dot(a, b, preferred_element_type=F32)


def _gelu_tanh(x):
    return 0.5 * x * (1.0 + jnp.tanh(math.sqrt(2.0 / math.pi) * (x + 0.044715 * (x * x * x))))


def _sigmoid(x):
    return 0.5 * jnp.tanh(0.5 * x) + 0.5


def _in_proj_lru_kernel(x_ref, w_ref, cs_ref, cw_ref, cb_ref, wg_ref, ba_ref, bx_ref, lam_ref,
                        qkv_ref, rec_ref, xext_ref, a_ref, u_ref, h_ref, *, chunk):
    nbatch, tt, d = x_ref.shape
    rows = nbatch * tt
    lru_w = rec_ref.shape[2]
    n_qkv = qkv_ref.shape[2]
    bw = wg_ref.shape[1]
    halo = V7X_SUBLANES
    pitch = xext_ref.shape[1]
    half = V7X_SUBLANES

    @pl.when(pl.program_id(0) == 0)
    def _():
        xext_ref[:, 0:halo, :] = jnp.zeros((nbatch, halo, lru_w), F32)
        h_ref[...] = jnp.zeros(h_ref.shape, F32)

    xb = x_ref[...].reshape(rows, d).astype(BF16)
    xext_ref[:, halo:halo + tt, :] = _dot(xb, w_ref[:, 0:lru_w]).reshape(nbatch, tt, lru_w)
    yg = _dot(xb, w_ref[:, lru_w:2 * lru_w])
    for c0 in range(0, n_qkv, chunk):
        val = _dot(xb, w_ref[:, 2 * lru_w + c0:2 * lru_w + c0 + chunk]) * cs_ref[:, c0:c0 + chunk]
        qkv_ref[:, :, c0:c0 + chunk] = val.astype(BF16).reshape(nbatch, tt, chunk)

    for n in range(lru_w // bw):
        cols = slice(n * bw, (n + 1) * bw)
        xc = cb_ref[:, cols]
        for t in range(CONV_W):
            start = halo - (CONV_W - 1) + t
            xc = xc + xext_ref[:, start:start + tt, cols].reshape(rows, bw) * cw_ref[t:t + 1, cols]
        gates = _dot(xc.astype(BF16), wg_ref[n])
        gate_a = _sigmoid(gates[:, :bw] + ba_ref[:, cols])
        gate_x = _sigmoid(gates[:, bw:] + bx_ref[:, cols])
        neg_lam = -lam_ref[:, cols]
        softplus = jnp.maximum(neg_lam, 0.0) + jnp.log1p(jnp.exp(-jnp.abs(neg_lam)))
        log_a = -LRU_C * gate_a * softplus
        a = jnp.exp(log_a)
        one_minus_a2 = -jnp.tanh(log_a) * (a * a + 1.0)
        mult = jnp.where(one_minus_a2 > 0.0, one_minus_a2 * lax.rsqrt(one_minus_a2), 0.0)
        u = mult * (gate_x * xc)
        for bi in range(nbatch):
            a_ref[n, bi * pitch:bi * pitch + tt, :] = a[bi * tt:(bi + 1) * tt]
            u_ref[n, bi * pitch:bi * pitch + tt, :] = u[bi * tt:(bi + 1) * tt]

        for g in range(nbatch // half):
            h = h_ref[g * half:(g + 1) * half, cols]
            for t in range(tt):
                step = pl.ds(g * half * pitch + t, half, stride=pitch)
                h = a_ref[n, step, :] * h + u_ref[n, step, :]
                u_ref[n, step, :] = h
            h_ref[g * half:(g + 1) * half, cols] = h

        hs = jnp.concatenate([u_ref[n, bi * pitch:bi * pitch + tt, :] for bi in range(nbatch)], axis=0)
        rec_ref[:, :, cols] = (hs * _gelu_tanh(yg[:, cols])).astype(BF16).reshape(nbatch, tt, bw)

    xext_ref[:, 0:halo, :] = xext_ref[:, tt:tt + halo, :]


def _in_proj_lru(x, w_bf16, qkv_scale, conv_w, conv_b, wgate, ba, bx, lam):
    nbatch, seq, d = x.shape
    nblk, bw, _ = wgate.shape
    lru_w = nblk * bw
    n_in = w_bf16.shape[1]
    n_qkv = n_in - 2 * lru_w
    tt = _tile(seq, 32)
    rows = nbatch * tt
    pitch = tt + V7X_SUBLANES
    chunk = _tile(n_qkv, 1024)
    assert nbatch % V7X_SUBLANES == 0 and tt % BF16_SUBLANES == 0
    est = (d * n_in * 2 + 2 * rows * d * 4 + 2 * rows * (n_qkv + lru_w) * 2 + 3 * nbatch * pitch * lru_w * 4
           + rows * d * 2 + 3 * rows * lru_w * 4 + 2 * rows * chunk * 4)
    fixed = lambda t: (0, 0)
    kern = functools.partial(_in_proj_lru_kernel, chunk=chunk)
    return pl.pallas_call(
        kern,
        grid=(seq // tt,),
        in_specs=[pl.BlockSpec((nbatch, tt, d), lambda t: (0, t, 0)),
                  pl.BlockSpec((d, n_in), fixed, pipeline_mode=pl.Buffered(1)),
                  pl.BlockSpec((1, n_qkv), fixed),
                  pl.BlockSpec((CONV_W, lru_w), fixed),
                  pl.BlockSpec((1, lru_w), fixed),
                  pl.BlockSpec((nblk, bw, 2 * bw), lambda t: (0, 0, 0)),
                  pl.BlockSpec((1, lru_w), fixed),
                  pl.BlockSpec((1, lru_w), fixed),
                  pl.BlockSpec((1, lru_w), fixed)],
        out_specs=[pl.BlockSpec((nbatch, tt, n_qkv), lambda t: (0, t, 0)),
                   pl.BlockSpec((nbatch, tt, lru_w), lambda t: (0, t, 0))],
        out_shape=[jax.ShapeDtypeStruct((nbatch, seq, n_qkv), BF16),
                   jax.ShapeDtypeStruct((nbatch, seq, lru_w), BF16)],
        scratch_shapes=[pltpu.VMEM((nbatch, pitch, lru_w), F32),
                        pltpu.VMEM((nblk, nbatch * pitch, bw), F32),
                        pltpu.VMEM((nblk, nbatch * pitch, bw), F32),
                        pltpu.VMEM((nbatch, lru_w), F32)],
        compiler_params=pltpu.CompilerParams(
            dimension_semantics=("arbitrary",), vmem_limit_bytes=_vmem_limit(est)),
        name="in_proj_rg_lru",
    )(x, w_bf16, qkv_scale, conv_w, conv_b, wgate, ba, bx, lam)


def _bucket_thresholds():
    n = np.arange(0, 8 * REL_MAX_DIST)
    max_exact = REL_BUCKETS // 2
    nf = np.maximum(n, 1).astype(np.float32)
    large = max_exact + (np.log(nf / np.float32(max_exact)) / np.float32(math.log(REL_MAX_DIST / max_exact))
                         * np.float32(REL_BUCKETS - max_exact)).astype(np.int32)
    bucket = np.where(n < max_exact, n, np.minimum(large, REL_BUCKETS - 1))
    assert np.all(np.diff(bucket) >= 0) and bucket[-1] == REL_BUCKETS - 1
    return [int(np.argmax(bucket >= j)) for j in range(1, REL_BUCKETS)]


def _near_key_blocks(tq):
    last = _bucket_thresholds()[-1]
    d = 0
    while d * tq - (tq - 1) < last:
        d += 1
    return d


def _bias_kernel(table_ref, o_ref, *, thresholds):
    h = pl.program_id(0)
    nb, tk, tq = o_ref.shape[1], o_ref.shape[2], o_ref.shape[3]
    key = lax.broadcasted_iota(jnp.int32, (tk, tq), 0)
    qry = lax.broadcasted_iota(jnp.int32, (tk, tq), 1)
    far = table_ref[REL_BUCKETS - 1, h]
    for d in range(nb):
        dist = d * tk + qry - key
        val = jnp.full((tk, tq), table_ref[0, h], F32)
        for j, thr in enumerate(thresholds):
            val = jnp.where(dist >= thr, table_ref[j + 1, h], val)
        o_ref[0, d] = jnp.where(dist < 0, -jnp.inf, (val - far) * LOG2_E)


def _bias_tiles(rel_bias, tq):
    heads = rel_bias.shape[1]
    nb = _near_key_blocks(tq)
    return pl.pallas_call(
        functools.partial(_bias_kernel, thresholds=_bucket_thresholds()),
        grid=(heads,),
        in_specs=[pl.BlockSpec(memory_space=pltpu.SMEM)],
        out_specs=pl.BlockSpec((1, nb, tq, tq), lambda h: (h, 0, 0, 0)),
        out_shape=jax.ShapeDtypeStruct((heads, nb, tq, tq), F32),
        compiler_params=pltpu.CompilerParams(
            dimension_semantics=("parallel",), vmem_limit_bytes=_vmem_limit(2 * nb * tq * tq * 4)),
        name="rel_bias_tiles",
    )(rel_bias)


def _attn_kernel(q_ref, k_ref, v_ref, bias_ref, lqk_ref, g_ref, o_ref, *, tq, lam_init):
    seq = q_ref.shape[1]
    nb = bias_ref.shape[1]
    nt_dims = (((1,), (1,)), ((), ()))

    lqk = lqk_ref[...]
    lam = (jnp.exp(jnp.sum(lqk[0:1] * lqk[1:2], axis=-1, keepdims=True))
           - jnp.exp(jnp.sum(lqk[2:3] * lqk[3:4], axis=-1, keepdims=True)) + lam_init)
    ones_rows = (lax.broadcasted_iota(jnp.int32, (BF16_SUBLANES, seq), 0) == 0).astype(F32)
    v_aug_t = jnp.concatenate([v_ref[0].astype(F32).T, ones_rows], axis=0).astype(BF16)
    lane = lax.broadcasted_iota(jnp.int32, (tq, ATT_DV), 1)

    def scores(i):
        q = q_ref[0, i * tq:(i + 1) * tq, :]
        k = k_ref[0, 0:(i + 1) * tq, :]
        n_far = max(i + 1 - nb, 0)
        out = []
        for c in range(2):
            q_c = jnp.where((lane >= ATT_DQK) if c else (lane < ATT_DQK), q, jnp.zeros_like(q))
            s_t = lax.dot_general(k, q_c, nt_dims, preferred_element_type=F32)
            parts = [s_t[0:n_far * tq]] if n_far else []
            for j in range(n_far, i + 1):
                parts.append(s_t[j * tq:(j + 1) * tq] + bias_ref[0, i - j])
            m = functools.reduce(jnp.maximum, [jnp.max(part, axis=0, keepdims=True) for part in parts])
            out.append((parts, m))
        return out

    n_q = seq // tq
    queue = [scores(i) for i in range(min(ATT_PIPELINE_DEPTH, n_q))]
    for i in range(n_q):
        current = queue.pop(0)
        if i + ATT_PIPELINE_DEPTH < n_q:
            queue.append(scores(i + ATT_PIPELINE_DEPTH))
        n_keys = (i + 1) * tq
        probs = [jnp.concatenate([jnp.exp2(part - m).astype(BF16) for part in parts], axis=0) for parts, m in current]
        pv = _dot(v_aug_t[:, 0:n_keys], jnp.concatenate(probs, axis=1))
        num1, num2 = pv[0:ATT_DV, 0:tq], pv[0:ATT_DV, tq:]
        sum1, sum2 = pv[ATT_DV:ATT_DV + 1, 0:tq], pv[ATT_DV:ATT_DV + 1, tq:]
        out_t = num1 * (1.0 / sum1) - num2 * (lam / sum2)
        rms = lax.rsqrt(jnp.mean(out_t * out_t, axis=0, keepdims=True) + LN_EPS)
        o_ref[0, i * tq:(i + 1) * tq, :] = ((out_t * rms).T * g_ref[...] * (1.0 - lam_init)).astype(o_ref.dtype)


def _diff_attention(proj3d, bias_tiles, lqk, subln_g, *, heads, qkv_col0, tq, lam_init):
    b, s, _ = proj3d.shape
    nb = bias_tiles.shape[1]
    q_blk0 = qkv_col0 // ATT_DV
    k_blk0 = q_blk0 + heads
    v_blk0 = k_blk0 + heads
    live_q_blocks = 4
    est = 2 * 4 * s * ATT_DV * 2 + 2 * nb * tq * tq * 4 + live_q_blocks * 2 * 2 * s * tq * 4
    kern = functools.partial(_attn_kernel, tq=tq, lam_init=lam_init)
    return pl.pallas_call(
        kern,
        grid=(b, heads),
        in_specs=[pl.BlockSpec((1, s, ATT_DV), lambda bi, h: (bi, 0, q_blk0 + h)),
                  pl.BlockSpec((1, s, ATT_DV), lambda bi, h: (bi, 0, k_blk0 + h)),
                  pl.BlockSpec((1, s, ATT_DV), lambda bi, h: (bi, 0, v_blk0 + h)),
                  pl.BlockSpec((1, nb, tq, tq), lambda bi, h: (h, 0, 0, 0)),
                  pl.BlockSpec(lqk.shape, lambda bi, h: (0, 0)),
                  pl.BlockSpec((1, ATT_DV), lambda bi, h: (0, 0))],
        out_specs=pl.BlockSpec((1, s, ATT_DV), lambda bi, h: (bi, 0, h)),
        out_shape=jax.ShapeDtypeStruct((b, s, heads * ATT_DV), BF16),
        compiler_params=pltpu.CompilerParams(
            dimension_semantics=("parallel", "parallel"), vmem_limit_bytes=_vmem_limit(est)),
        name="diff_attention",
    )(proj3d, proj3d, proj3d, bias_tiles, lqk, subln_g)


def _out_proj_kernel(x_ref, rec_ref, att_ref, w_ref, g_ref, b_ref, o_ref, ob_ref, *, alpha):
    w1 = rec_ref.shape[1]
    m = _dot(rec_ref[...], w_ref[0:w1, :]) + _dot(att_ref[...], w_ref[w1:, :])
    h = _layer_norm(alpha * x_ref[...] + m, g_ref[...], b_ref[...])
    o_ref[...] = h
    ob_ref[...] = h.astype(BF16)


def _out_proj_ln(x2d, rec2d, att2d, w_bf16, g, b, alpha):
    m, d = x2d.shape
    w1, w2 = rec2d.shape[1], att2d.shape[1]
    tm = _tile(m, 512)
    est = 2 * tm * d * 4 * 2 + 2 * tm * (w1 + w2) * 2 + 2 * (w1 + w2) * d * 2 + 2 * tm * d * 4 + 2 * tm * d * 2
    rowblk = lambda i: (i, 0)
    fixed = lambda i: (0, 0)
    return pl.pallas_call(
        functools.partial(_out_proj_kernel, alpha=alpha),
        grid=(m // tm,),
        in_specs=[pl.BlockSpec((tm, d), rowblk), pl.BlockSpec((tm, w1), rowblk), pl.BlockSpec((tm, w2), rowblk),
                  pl.BlockSpec((w1 + w2, d), fixed), pl.BlockSpec((1, d), fixed), pl.BlockSpec((1, d), fixed)],
        out_specs=[pl.BlockSpec((tm, d), rowblk), pl.BlockSpec((tm, d), rowblk)],
        out_shape=[jax.ShapeDtypeStruct((m, d), F32), jax.ShapeDtypeStruct((m, d), BF16)],
        compiler_params=pltpu.CompilerParams(
            dimension_semantics=("parallel",), vmem_limit_bytes=_vmem_limit(est)),
        name="out_proj_ln1",
    )(x2d, rec2d, att2d, w_bf16, g, b)


def _ffn_kernel(hb_ref, wg_ref, wu_ref, wd_ref, o_ref):
    @pl.when(pl.program_id(1) == 0)
    def _():
        o_ref[...] = jnp.zeros(o_ref.shape, F32)

    hb = hb_ref[...]
    gate = _dot(hb, wg_ref[...])
    up = _dot(hb, wu_ref[...])
    act = (gate * jax.nn.sigmoid(gate)) * up
    o_ref[...] += _dot(act.astype(BF16), wd_ref[...])


def _ffn(hb2d, wg, wu, wd):
    m, d = hb2d.shape
    f = wg.shape[1]
    tm, tf = _tile(m, 1024), _tile(f, 512)
    est = 2 * tm * d * 2 + 2 * 3 * d * tf * 2 + 2 * tm * d * 4 + 3 * tm * tf * 4 + tm * d * 4
    rowblk = lambda i, j: (i, 0)
    return pl.pallas_call(
        _ffn_kernel,
        grid=(m // tm, f // tf),
        in_specs=[pl.BlockSpec((tm, d), rowblk),
                  pl.BlockSpec((d, tf), lambda i, j: (0, j)),
                  pl.BlockSpec((d, tf), lambda i, j: (0, j)),
                  pl.BlockSpec((tf, d), lambda i, j: (j, 0))],
        out_specs=pl.BlockSpec((tm, d), rowblk),
        out_shape=jax.ShapeDtypeStruct((m, d), F32),
        compiler_params=pltpu.CompilerParams(
            dimension_semantics=("parallel", "arbitrary"), vmem_limit_bytes=_vmem_limit(est)),
        name="swiglu",
    )(hb2d, wg, wu, wd)


def _ple_kernel(h_ref, f_ref, p_ref, g2_ref, b2_ref, wg_ref, bg_ref, wp_ref, g3_ref, b3_ref, o_ref, *, alpha):
    h = _layer_norm(alpha * h_ref[...] + f_ref[...], g2_ref[...], b2_ref[...])
    gate = jax.nn.sigmoid(_dot(h.astype(BF16), wg_ref[...]) + bg_ref[...])
    e = _dot(p_ref[...].astype(BF16), wp_ref[...])
    o_ref[...] = _layer_norm(alpha * h + gate * e, g3_ref[...], b3_ref[...])


def _ln2_ple_ln3(h2d, f2d, p2d, g2, b2, wg, bg, wp, g3, b3, alpha):
    m, d = h2d.shape
    pd = p2d.shape[1]
    tm = _tile(m, 512)
    est = 3 * 2 * tm * d * 4 + 2 * tm * pd * 4 + 2 * d * d * 2 + 2 * pd * d * 2 + 3 * tm * d * 4
    rowblk = lambda i: (i, 0)
    fixed = lambda i: (0, 0)
    vec = pl.BlockSpec((1, d), fixed)
    return pl.pallas_call(
        functools.partial(_ple_kernel, alpha=alpha),
        grid=(m // tm,),
        in_specs=[pl.BlockSpec((tm, d), rowblk), pl.BlockSpec((tm, d), rowblk), pl.BlockSpec((tm, pd), rowblk),
                  vec, vec, pl.BlockSpec((d, d), fixed), vec, pl.BlockSpec((pd, d), fixed), vec, vec],
        out_specs=pl.BlockSpec((tm, d), rowblk),
        out_shape=jax.ShapeDtypeStruct((m, d), F32),
        compiler_params=pltpu.CompilerParams(
            dimension_semantics=("parallel",), vmem_limit_bytes=_vmem_limit(est)),
        name="ln2_ple_ln3",
    )(h2d, f2d, p2d, g2, b2, wg, bg, wp, g3, b3)


def kernel(x, p, w_in, conv_w, conv_b, lru_wa, lru_ba, lru_wx, lru_bx, lru_lambda, diff_lq1, diff_lk1, diff_lq2, diff_lk2, diff_subln_g, rel_bias, w_out, ln1_g, ln1_b, w_ffn_gate, w_ffn_up, w_ffn_down, ln2_g, ln2_b, w_ple_gate, b_ple_gate, w_ple_proj, ln3_g, ln3_b):
    batch, seq, d = x.shape
    depth = w_in.shape[0]
    nblk, bw = lru_wa.shape[1], lru_wa.shape[2]
    lru_w = nblk * bw
    heads = rel_bias.shape[1]
    att_qk = 2 * heads * ATT_DQK
    assert w_in.shape[2] == 2 * lru_w + 2 * att_qk + heads * ATT_DV
    assert lru_w % ATT_DV == 0 and att_qk == heads * ATT_DV
    m = batch * seq
    alpha = (2 * depth) ** 0.25

    tq = _tile(seq, 256)
    bias_tiles = _bias_tiles(rel_bias, tq)
    row = lambda v: v.reshape(1, -1)
    col = jnp.arange(2 * att_qk + heads * ATT_DV)
    qkv_scale = row(jnp.where(col < att_qk, ATT_DQK ** -0.5 * LOG2_E, 1.0).astype(F32))

    h = x.reshape(m, d)
    for i in range(depth):
        lam_init = 0.8 - 0.6 * math.exp(-0.3 * i)
        wgate = jnp.concatenate([lru_wa[i], lru_wx[i]], axis=-1).astype(BF16)
        qkv, rec = _in_proj_lru(h.reshape(batch, seq, d), w_in[i].astype(BF16), qkv_scale, conv_w[i], row(conv_b[i]),
                                wgate, row(lru_ba[i]), row(lru_bx[i]), row(lru_lambda[i]))
        lqk = jnp.stack([diff_lq1[i], diff_lk1[i], diff_lq2[i], diff_lk2[i]])
        att = _diff_attention(qkv, bias_tiles, lqk, row(diff_subln_g[i]), heads=heads, qkv_col0=0,
                              tq=tq, lam_init=lam_init)
        h, hb = _out_proj_ln(h, rec.reshape(m, lru_w), att.reshape(m, heads * ATT_DV), w_out[i].astype(BF16),
                             row(ln1_g[i]), row(ln1_b[i]), alpha)
        f = _ffn(hb, w_ffn_gate[i].astype(BF16), w_ffn_up[i].astype(BF16), w_ffn_down[i].astype(BF16))
        h = _ln2_ple_ln3(h, f, p[i].reshape(m, -1), row(ln2_g[i]), row(ln2_b[i]), w_ple_gate[i].astype(BF16),
                         row(b_ple_gate[i]), w_ple_proj[i].astype(BF16), row(ln3_g[i]), row(ln3_b[i]), alpha)
    return h.reshape(batch, seq, d)
```

```python
import functools
import math

import jax
import jax.numpy as jnp
import numpy as np
from jax import lax
from jax.experimental import pallas as pl
from jax.experimental.pallas import tpu as pltpu

F32 = jnp.float32
BF16 = jnp.bfloat16

LN_EPS = 1e-5
LRU_C = 8.0
CONV_W = 4
ATT_DV = 128
ATT_DQK = ATT_DV // 2
REL_BUCKETS = 32
REL_MAX_DIST = 128
ATT_PIPELINE_DEPTH = 2
LOG2_E = math.log2(math.e)

V7X_LANES = 128
V7X_SUBLANES = 8
BF16_SUBLANES = 2 * V7X_SUBLANES
V7X_VMEM_BYTES = 64 * 1024 * 1024
V7X_VMEM_RESERVE_BYTES = 6 * 1024 * 1024


def _vmem_limit(estimate_bytes):
    return int(min(V7X_VMEM_BYTES - V7X_VMEM_RESERVE_BYTES, max(2 * estimate_bytes, 16 * 1024 * 1024)))


def _tile(dim, preferred):
    if dim % preferred == 0:
        return preferred
    return dim


def _layer_norm(y, g, b):
    mu = jnp.mean(y, axis=-1, keepdims=True)
    yc = y - mu
    var = jnp.mean(yc * yc, axis=-1, keepdims=True)
    return yc * lax.rsqrt(var + LN_EPS) * g + b


def _dot(a, b):
    return jnp.dot(a, b, preferred_element_type=F32)


def _gelu_tanh(x):
    return 0.5 * x * (1.0 + jnp.tanh(math.sqrt(2.0 / math.pi) * (x + 0.044715 * (x * x * x))))


def _sigmoid(x):
    return 0.5 * jnp.tanh(0.5 * x) + 0.5


def _in_proj_lru_kernel(x_ref, w_ref, cs_ref, cw_ref, cb_ref, wg_ref, ba_ref, bx_ref, lam_ref,
                        qkv_ref, rec_ref, xext_ref, a_ref, u_ref, h_ref, *, chunk):
    nbatch, tt, d = x_ref.shape
    rows = nbatch * tt
    lru_w = rec_ref.shape[2]
    n_qkv = qkv_ref.shape[1] * qkv_ref.shape[3]
    bw = wg_ref.shape[1]
    halo = V7X_SUBLANES
    pitch = xext_ref.shape[1]
    half = V7X_SUBLANES

    @pl.when(pl.program_id(0) == 0)
    def _():
        xext_ref[:, 0:halo, :] = jnp.zeros((nbatch, halo, lru_w), F32)
        h_ref[...] = jnp.zeros(h_ref.shape, F32)

    xb = x_ref[...].reshape(rows, d).astype(BF16)
    xext_ref[:, halo:halo + tt, :] = _dot(xb, w_ref[:, 0:lru_w]).reshape(nbatch, tt, lru_w)
    yg = _dot(xb, w_ref[:, lru_w:2 * lru_w])
    for c0 in range(0, n_qkv, chunk):
        val = _dot(xb, w_ref[:, 2 * lru_w + c0:2 * lru_w + c0 + chunk]) * cs_ref[:, c0:c0 + chunk]
        val = val.astype(BF16)
        for c in range(0, chunk, ATT_DV):
            qkv_ref[:, (c0 + c) // ATT_DV, :, :] = val[:, c:c + ATT_DV].reshape(nbatch, tt, ATT_DV)

    for n in range(lru_w // bw):
        cols = slice(n * bw, (n + 1) * bw)
        xc = cb_ref[:, cols]
        for t in range(CONV_W):
            start = halo - (CONV_W - 1) + t
            xc = xc + xext_ref[:, start:start + tt, cols].reshape(rows, bw) * cw_ref[t:t + 1, cols]
        gates = _dot(xc.astype(BF16), wg_ref[n])
        gate_a = _sigmoid(gates[:, :bw] + ba_ref[:, cols])
        gate_x = _sigmoid(gates[:, bw:] + bx_ref[:, cols])
        neg_lam = -lam_ref[:, cols]
        softplus = jnp.maximum(neg_lam, 0.0) + jnp.log1p(jnp.exp(-jnp.abs(neg_lam)))
        log_a = -LRU_C * gate_a * softplus
        a = jnp.exp(log_a)
        one_minus_a2 = -jnp.tanh(log_a) * (a * a + 1.0)
        mult = jnp.where(one_minus_a2 > 0.0, one_minus_a2 * lax.rsqrt(one_minus_a2), 0.0)
        u = mult * (gate_x * xc)
        for bi in range(nbatch):
            a_ref[n, bi * pitch:bi * pitch + tt, :] = a[bi * tt:(bi + 1) * tt]
            u_ref[n, bi * pitch:bi * pitch + tt, :] = u[bi * tt:(bi + 1) * tt]

        for g in range(nbatch // half):
            h = h_ref[g * half:(g + 1) * half, cols]
            for t in range(tt):
                step = pl.ds(g * half * pitch + t, half, stride=pitch)
                h = a_ref[n, step, :] * h + u_ref[n, step, :]
                u_ref[n, step, :] = h
            h_ref[g * half:(g + 1) * half, cols] = h

        hs = jnp.concatenate([u_ref[n, bi * pitch:bi * pitch + tt, :] for bi in range(nbatch)], axis=0)
        rec_ref[:, :, cols] = (hs * _gelu_tanh(yg[:, cols])).astype(BF16).reshape(nbatch, tt, bw)

    xext_ref[:, 0:halo, :] = xext_ref[:, tt:tt + halo, :]


def _in_proj_lru(x, w_bf16, qkv_scale, conv_w, conv_b, wgate, ba, bx, lam):
    nbatch, seq, d = x.shape
    nblk, bw, _ = wgate.shape
    lru_w = nblk * bw
    n_in = w_bf16.shape[1]
    n_qkv = n_in - 2 * lru_w
    tt = _tile(seq, 32)
    rows = nbatch * tt
    pitch = tt + V7X_SUBLANES
    chunk = _tile(n_qkv, 1024)
    assert nbatch % V7X_SUBLANES == 0 and tt % BF16_SUBLANES == 0
    est = (d * n_in * 2 + 2 * rows * d * 4 + 2 * rows * (n_qkv + lru_w) * 2 + 3 * nbatch * pitch * lru_w * 4
           + rows * d * 2 + 3 * rows * lru_w * 4 + 2 * rows * chunk * 4)
    fixed = lambda t: (0, 0)
    kern = functools.partial(_in_proj_lru_kernel, chunk=chunk)
    return pl.pallas_call(
        kern,
        grid=(seq // tt,),
        in_specs=[pl.BlockSpec((nbatch, tt, d), lambda t: (0, t, 0)),
                  pl.BlockSpec((d, n_in), fixed, pipeline_mode=pl.Buffered(1)),
                  pl.BlockSpec((1, n_qkv), fixed),
                  pl.BlockSpec((CONV_W, lru_w), fixed),
                  pl.BlockSpec((1, lru_w), fixed),
                  pl.BlockSpec((nblk, bw, 2 * bw), lambda t: (0, 0, 0)),
                  pl.BlockSpec((1, lru_w), fixed),
                  pl.BlockSpec((1, lru_w), fixed),
                  pl.BlockSpec((1, lru_w), fixed)],
        out_specs=[pl.BlockSpec((nbatch, n_qkv // ATT_DV, tt, ATT_DV), lambda t: (0, 0, t, 0)),
                   pl.BlockSpec((nbatch, tt, lru_w), lambda t: (0, t, 0))],
        out_shape=[jax.ShapeDtypeStruct((nbatch, n_qkv // ATT_DV, seq, ATT_DV), BF16),
                   jax.ShapeDtypeStruct((nbatch, seq, lru_w), BF16)],
        scratch_shapes=[pltpu.VMEM((nbatch, pitch, lru_w), F32),
                        pltpu.VMEM((nblk, nbatch * pitch, bw), F32),
                        pltpu.VMEM((nblk, nbatch * pitch, bw), F32),
                        pltpu.VMEM((nbatch, lru_w), F32)],
        compiler_params=pltpu.CompilerParams(
            dimension_semantics=("arbitrary",), vmem_limit_bytes=_vmem_limit(est)),
        name="in_proj_rg_lru",
    )(x, w_bf16, qkv_scale, conv_w, conv_b, wgate, ba, bx, lam)


def _bucket_thresholds():
    n = np.arange(0, 8 * REL_MAX_DIST)
    max_exact = REL_BUCKETS // 2
    nf = np.maximum(n, 1).astype(np.float32)
    large = max_exact + (np.log(nf / np.float32(max_exact)) / np.float32(math.log(REL_MAX_DIST / max_exact))
                         * np.float32(REL_BUCKETS - max_exact)).astype(np.int32)
    bucket = np.where(n < max_exact, n, np.minimum(large, REL_BUCKETS - 1))
    assert np.all(np.diff(bucket) >= 0) and bucket[-1] == REL_BUCKETS - 1
    return [int(np.argmax(bucket >= j)) for j in range(1, REL_BUCKETS)]


def _near_key_blocks(tq):
    last = _bucket_thresholds()[-1]
    d = 0
    while d * tq - (tq - 1) < last:
        d += 1
    return d


def _bias_kernel(table_ref, o_ref, *, thresholds):
    h = pl.program_id(0)
    nb, tk, tq = o_ref.shape[1], o_ref.shape[2], o_ref.shape[3]
    key = lax.broadcasted_iota(jnp.int32, (tk, tq), 0)
    qry = lax.broadcasted_iota(jnp.int32, (tk, tq), 1)
    far = table_ref[REL_BUCKETS - 1, h]
    for d in range(nb):
        dist = d * tk + qry - key
        val = jnp.full((tk, tq), table_ref[0, h], F32)
        for j, thr in enumerate(thresholds):
            val = jnp.where(dist >= thr, table_ref[j + 1, h], val)
        o_ref[0, d] = jnp.where(dist < 0, -jnp.inf, (val - far) * LOG2_E)


def _bias_tiles(rel_bias, tq):
    heads = rel_bias.shape[1]
    nb = _near_key_blocks(tq)
    return pl.pallas_call(
        functools.partial(_bias_kernel, thresholds=_bucket_thresholds()),
        grid=(heads,),
        in_specs=[pl.BlockSpec(memory_space=pltpu.SMEM)],
        out_specs=pl.BlockSpec((1, nb, tq, tq), lambda h: (h, 0, 0, 0)),
        out_shape=jax.ShapeDtypeStruct((heads, nb, tq, tq), F32),
        compiler_params=pltpu.CompilerParams(
            dimension_semantics=("parallel",), vmem_limit_bytes=_vmem_limit(2 * nb * tq * tq * 4)),
        name="rel_bias_tiles",
    )(rel_bias)


def _attn_kernel(q_ref, k_ref, v_ref, bias_ref, lqk_ref, g_ref, o_ref, *, tq, lam_init):
    seq = q_ref.shape[2]
    nb = bias_ref.shape[1]
    nt_dims = (((1,), (1,)), ((), ()))

    lqk = lqk_ref[...]
    lam = (jnp.exp(jnp.sum(lqk[0:1] * lqk[1:2], axis=-1, keepdims=True))
           - jnp.exp(jnp.sum(lqk[2:3] * lqk[3:4], axis=-1, keepdims=True)) + lam_init)
    ones_rows = (lax.broadcasted_iota(jnp.int32, (BF16_SUBLANES, seq), 0) == 0).astype(F32)
    v_aug_t = jnp.concatenate([v_ref[0, 0].astype(F32).T, ones_rows], axis=0).astype(BF16)
    lane = lax.broadcasted_iota(jnp.int32, (tq, ATT_DV), 1)

    def scores(i):
        q = q_ref[0, 0, i * tq:(i + 1) * tq, :]
        k = k_ref[0, 0, 0:(i + 1) * tq, :]
        n_far = max(i + 1 - nb, 0)
        out = []
        for c in range(2):
            q_c = jnp.where((lane >= ATT_DQK) if c else (lane < ATT_DQK), q, jnp.zeros_like(q))
            s_t = lax.dot_general(k, q_c, nt_dims, preferred_element_type=F32)
            parts = [s_t[0:n_far * tq]] if n_far else []
            for j in range(n_far, i + 1):
                parts.append(s_t[j * tq:(j + 1) * tq] + bias_ref[0, i - j])
            m = functools.reduce(jnp.maximum, [jnp.max(part, axis=0, keepdims=True) for part in parts])
            out.append((parts, m))
        return out

    n_q = seq // tq
    queue = [scores(i) for i in range(min(ATT_PIPELINE_DEPTH, n_q))]
    for i in range(n_q):
        current = queue.pop(0)
        if i + ATT_PIPELINE_DEPTH < n_q:
            queue.append(scores(i + ATT_PIPELINE_DEPTH))
        n_keys = (i + 1) * tq
        probs = [jnp.concatenate([jnp.exp2(part - m).astype(BF16) for part in parts], axis=0) for parts, m in current]
        pv = _dot(v_aug_t[:, 0:n_keys], jnp.concatenate(probs, axis=1))
        num1, num2 = pv[0:ATT_DV, 0:tq], pv[0:ATT_DV, tq:]
        sum1, sum2 = pv[ATT_DV:ATT_DV + 1, 0:tq], pv[ATT_DV:ATT_DV + 1, tq:]
        out_t = num1 * (1.0 / sum1) - num2 * (lam / sum2)
        rms = lax.rsqrt(jnp.mean(out_t * out_t, axis=0, keepdims=True) + LN_EPS)
        o_ref[0, 0, i * tq:(i + 1) * tq, :] = ((out_t * rms).T * g_ref[...] * (1.0 - lam_init)).astype(o_ref.dtype)


def _diff_attention(qkv, bias_tiles, lqk, subln_g, *, heads, tq, lam_init):
    b, _, s, _ = qkv.shape
    nb = bias_tiles.shape[1]
    live_q_blocks = 4
    est = 2 * 4 * s * ATT_DV * 2 + 2 * nb * tq * tq * 4 + live_q_blocks * 2 * 2 * s * tq * 4
    kern = functools.partial(_attn_kernel, tq=tq, lam_init=lam_init)
    head_blk = (1, 1, s, ATT_DV)
    return pl.pallas_call(
        kern,
        grid=(b, heads),
        in_specs=[pl.BlockSpec(head_blk, lambda bi, h: (bi, h, 0, 0)),
                  pl.BlockSpec(head_blk, lambda bi, h: (bi, heads + h, 0, 0)),
                  pl.BlockSpec(head_blk, lambda bi, h: (bi, 2 * heads + h, 0, 0)),
                  pl.BlockSpec((1, nb, tq, tq), lambda bi, h: (h, 0, 0, 0)),
                  pl.BlockSpec(lqk.shape, lambda bi, h: (0, 0)),
                  pl.BlockSpec((1, ATT_DV), lambda bi, h: (0, 0))],
        out_specs=pl.BlockSpec(head_blk, lambda bi, h: (bi, h, 0, 0)),
        out_shape=jax.ShapeDtypeStruct((b, heads, s, ATT_DV), BF16),
        compiler_params=pltpu.CompilerParams(
            dimension_semantics=("parallel", "parallel"), vmem_limit_bytes=_vmem_limit(est)),
        name="diff_attention",
    )(qkv, qkv, qkv, bias_tiles, lqk, subln_g)


def _out_proj_kernel(x_ref, rec_ref, att_ref, w_ref, g_ref, b_ref, o_ref, ob_ref, *, alpha):
    w1 = rec_ref.shape[1]
    att = jnp.concatenate([att_ref[0, hd] for hd in range(att_ref.shape[1])], axis=1)
    m = _dot(rec_ref[...], w_ref[0:w1, :]) + _dot(att, w_ref[w1:, :])
    h = _layer_norm(alpha * x_ref[...] + m, g_ref[...], b_ref[...])
    o_ref[...] = h
    ob_ref[...] = h.astype(BF16)


def _out_proj_ln(x2d, rec2d, att4d, w_bf16, g, b, alpha):
    m, d = x2d.shape
    _, heads, seq, dv = att4d.shape
    w1, w2 = rec2d.shape[1], heads * dv
    tm = _tile(seq, 512)
    per_seq = seq // tm
    est = 2 * tm * d * 4 * 2 + 2 * tm * (w1 + w2) * 2 + 2 * (w1 + w2) * d * 2 + 2 * tm * d * 4 + 2 * tm * d * 2
    rowblk = lambda i: (i, 0)
    fixed = lambda i: (0, 0)
    return pl.pallas_call(
        functools.partial(_out_proj_kernel, alpha=alpha),
        grid=(m // tm,),
        in_specs=[pl.BlockSpec((tm, d), rowblk), pl.BlockSpec((tm, w1), rowblk),
                  pl.BlockSpec((1, heads, tm, dv), lambda i: (i // per_seq, 0, i % per_seq, 0)),
                  pl.BlockSpec((w1 + w2, d), fixed), pl.BlockSpec((1, d), fixed), pl.BlockSpec((1, d), fixed)],
        out_specs=[pl.BlockSpec((tm, d), rowblk), pl.BlockSpec((tm, d), rowblk)],
        out_shape=[jax.ShapeDtypeStruct((m, d), F32), jax.ShapeDtypeStruct((m, d), BF16)],
        compiler_params=pltpu.CompilerParams(
            dimension_semantics=("parallel",), vmem_limit_bytes=_vmem_limit(est)),
        name="out_proj_ln1",
    )(x2d, rec2d, att4d, w_bf16, g, b)


def _ffn_kernel(hb_ref, wg_ref, wu_ref, wd_ref, o_ref):
    @pl.when(pl.program_id(1) == 0)
    def _():
        o_ref[...] = jnp.zeros(o_ref.shape, F32)

    hb = hb_ref[...]
    gate = _dot(hb, wg_ref[...])
    up = _dot(hb, wu_ref[...])
    act = (gate * jax.nn.sigmoid(gate)) * up
    o_ref[...] += _dot(act.astype(BF16), wd_ref[...])


def _ffn(hb2d, wg, wu, wd):
    m, d = hb2d.shape
    f = wg.shape[1]
    tm, tf = _tile(m, 1024), _tile(f, 512)
    est = 2 * tm * d * 2 + 2 * 3 * d * tf * 2 + 2 * tm * d * 4 + 3 * tm * tf * 4 + tm * d * 4
    rowblk = lambda i, j: (i, 0)
    return pl.pallas_call(
        _ffn_kernel,
        grid=(m // tm, f // tf),
        in_specs=[pl.BlockSpec((tm, d), rowblk),
                  pl.BlockSpec((d, tf), lambda i, j: (0, j)),
                  pl.BlockSpec((d, tf), lambda i, j: (0, j)),
                  pl.BlockSpec((tf, d), lambda i, j: (j, 0))],
        out_specs=pl.BlockSpec((tm, d), rowblk),
        out_shape=jax.ShapeDtypeStruct((m, d), F32),
        compiler_params=pltpu.CompilerParams(
            dimension_semantics=("parallel", "arbitrary"), vmem_limit_bytes=_vmem_limit(est)),
        name="swiglu",
    )(hb2d, wg, wu, wd)


def _ple_kernel(h_ref, f_ref, p_ref, g2_ref, b2_ref, wg_ref, bg_ref, wp_ref, g3_ref, b3_ref, o_ref, *, alpha):
    h = _layer_norm(alpha * h_ref[...] + f_ref[...], g2_ref[...], b2_ref[...])
    gate = jax.nn.sigmoid(_dot(h.astype(BF16), wg_ref[...]) + bg_ref[...])
    e = _dot(p_ref[...].astype(BF16), wp_ref[...])
    o_ref[...] = _layer_norm(alpha * h + gate * e, g3_ref[...], b3_ref[...])


def _ln2_ple_ln3(h2d, f2d, p2d, g2, b2, wg, bg, wp, g3, b3, alpha):
    m, d = h2d.shape
    pd = p2d.shape[1]
    tm = _tile(m, 512)
    est = 3 * 2 * tm * d * 4 + 2 * tm * pd * 4 + 2 * d * d * 2 + 2 * pd * d * 2 + 3 * tm * d * 4
    rowblk = lambda i: (i, 0)
    fixed = lambda i: (0, 0)
    vec = pl.BlockSpec((1, d), fixed)
    return pl.pallas_call(
        functools.partial(_ple_kernel, alpha=alpha),
        grid=(m // tm,),
        in_specs=[pl.BlockSpec((tm, d), rowblk), pl.BlockSpec((tm, d), rowblk), pl.BlockSpec((tm, pd), rowblk),
                  vec, vec, pl.BlockSpec((d, d), fixed), vec, pl.BlockSpec((pd, d), fixed), vec, vec],
        out_specs=pl.BlockSpec((tm, d), rowblk),
        out_shape=jax.ShapeDtypeStruct((m, d), F32),
        compiler_params=pltpu.CompilerParams(
            dimension_semantics=("parallel",), vmem_limit_bytes=_vmem_limit(est)),
        name="ln2_ple_ln3",
    )(h2d, f2d, p2d, g2, b2, wg, bg, wp, g3, b3)


def kernel(x, p, w_in, conv_w, conv_b, lru_wa, lru_ba, lru_wx, lru_bx, lru_lambda, diff_lq1, diff_lk1, diff_lq2, diff_lk2, diff_subln_g, rel_bias, w_out, ln1_g, ln1_b, w_ffn_gate, w_ffn_up, w_ffn_down, ln2_g, ln2_b, w_ple_gate, b_ple_gate, w_ple_proj, ln3_g, ln3_b):
    batch, seq, d = x.shape
    depth = w_in.shape[0]
    nblk, bw = lru_wa.shape[1], lru_wa.shape[2]
    lru_w = nblk * bw
    heads = rel_bias.shape[1]
    att_qk = 2 * heads * ATT_DQK
    assert w_in.shape[2] == 2 * lru_w + 2 * att_qk + heads * ATT_DV
    assert lru_w % ATT_DV == 0 and att_qk == heads * ATT_DV
    m = batch * seq
    alpha = (2 * depth) ** 0.25

    tq = _tile(seq, 256)
    bias_tiles = _bias_tiles(rel_bias, tq)
    row = lambda v: v.reshape(1, -1)
    col = jnp.arange(2 * att_qk + heads * ATT_DV)
    qkv_scale = row(jnp.where(col < att_qk, ATT_DQK ** -0.5 * LOG2_E, 1.0).astype(F32))

    h = x.reshape(m, d)
    for i in range(depth):
        lam_init = 0.8 - 0.6 * math.exp(-0.3 * i)
        wgate = jnp.concatenate([lru_wa[i], lru_wx[i]], axis=-1).astype(BF16)
        qkv, rec = _in_proj_lru(h.reshape(batch, seq, d), w_in[i].astype(BF16), qkv_scale, conv_w[i], row(conv_b[i]),
                                wgate, row(lru_ba[i]), row(lru_bx[i]), row(lru_lambda[i]))
        lqk = jnp.stack([diff_lq1[i], diff_lk1[i], diff_lq2[i], diff_lk2[i]])
        att = _diff_attention(qkv, bias_tiles, lqk, row(diff_subln_g[i]), heads=heads, tq=tq, lam_init=lam_init)
        h, hb = _out_proj_ln(h, rec.reshape(m, lru_w), att, w_out[i].astype(BF16),
                             row(ln1_g[i]), row(ln1_b[i]), alpha)
        f = _ffn(hb, w_ffn_gate[i].astype(BF16), w_ffn_up[i].astype(BF16), w_ffn_down[i].astype(BF16))
        h = _ln2_ple_ln3(h, f, p[i].reshape(m, -1), row(ln2_g[i]), row(ln2_b[i]), w_ple_gate[i].astype(BF16),
                         row(b_ple_gate[i]), w_ple_proj[i].astype(BF16), row(ln3_g[i]), row(ln3_b[i]), alpha)
    return h.reshape(batch, seq, d)
```

```python
import functools
import math

import jax
import jax.numpy as jnp
import numpy as np
from jax import lax
from jax.experimental import pallas as pl
from jax.experimental.pallas import tpu as pltpu

F32 = jnp.float32
BF16 = jnp.bfloat16

LN_EPS = 1e-5
LRU_C = 8.0
CONV_W = 4
ATT_DV = 128
ATT_DQK = ATT_DV // 2
REL_BUCKETS = 32
REL_MAX_DIST = 128
ATT_PIPELINE_DEPTH = 2
LOG2_E = math.log2(math.e)

V7X_LANES = 128
V7X_SUBLANES = 8
BF16_SUBLANES = 2 * V7X_SUBLANES
V7X_VMEM_BYTES = 64 * 1024 * 1024
V7X_VMEM_RESERVE_BYTES = 6 * 1024 * 1024


def _vmem_limit(estimate_bytes):
    return int(min(V7X_VMEM_BYTES - V7X_VMEM_RESERVE_BYTES, max(2 * estimate_bytes, 16 * 1024 * 1024)))


def _tile(dim, preferred):
    if dim % preferred == 0:
        return preferred
    return dim


def _layer_norm(y, g, b):
    mu = jnp.mean(y, axis=-1, keepdims=True)
    yc = y - mu
    var = jnp.mean(yc * yc, axis=-1, keepdims=True)
    return yc * lax.rsqrt(var + LN_EPS) * g + b


def _dot(a, b):
    return jnp.dot(a, b, preferred_element_type=F32)


def _gelu_tanh(x):
    return 0.5 * x * (1.0 + jnp.tanh(math.sqrt(2.0 / math.pi) * (x + 0.044715 * (x * x * x))))


def _in_proj_lru_kernel(x_ref, w_ref, cs_ref, cw_ref, cb_ref, wg_ref, ba_ref, bx_ref, lam_ref,
                        qkv_ref, rec_ref, xext_ref, a_ref, u_ref, h_ref, *, chunk):
    nbatch, tt, d = x_ref.shape
    rows = nbatch * tt
    lru_w = rec_ref.shape[2]
    n_qkv = qkv_ref.shape[2]
    bw = wg_ref.shape[1]
    halo = V7X_SUBLANES
    pitch = xext_ref.shape[1]
    half = V7X_SUBLANES

    @pl.when(pl.program_id(0) == 0)
    def _():
        xext_ref[:, 0:halo, :] = jnp.zeros((nbatch, halo, lru_w), F32)
        h_ref[...] = jnp.zeros(h_ref.shape, F32)

    xb = x_ref[...].reshape(rows, d).astype(BF16)
    xext_ref[:, halo:halo + tt, :] = _dot(xb, w_ref[:, 0:lru_w]).reshape(nbatch, tt, lru_w)
    yg = _dot(xb, w_ref[:, lru_w:2 * lru_w])
    for c0 in range(0, n_qkv, chunk):
        val = _dot(xb, w_ref[:, 2 * lru_w + c0:2 * lru_w + c0 + chunk]) * cs_ref[:, c0:c0 + chunk]
        qkv_ref[:, :, c0:c0 + chunk] = val.astype(BF16).reshape(nbatch, tt, chunk)

    for n in range(lru_w // bw):
        cols = slice(n * bw, (n + 1) * bw)
        xc = cb_ref[:, cols]
        for t in range(CONV_W):
            start = halo - (CONV_W - 1) + t
            xc = xc + xext_ref[:, start:start + tt, cols].reshape(rows, bw) * cw_ref[t:t + 1, cols]
        gates = _dot(xc.astype(BF16), wg_ref[n])
        gate_a = jax.nn.sigmoid(gates[:, :bw] + ba_ref[:, cols])
        gate_x = jax.nn.sigmoid(gates[:, bw:] + bx_ref[:, cols])
        neg_lam = -lam_ref[:, cols]
        softplus = jnp.maximum(neg_lam, 0.0) + jnp.log1p(jnp.exp(-jnp.abs(neg_lam)))
        log_a = -LRU_C * gate_a * softplus
        a = jnp.exp(log_a)
        one_minus_a2 = -jnp.tanh(log_a) * (a * a + 1.0)
        mult = jnp.where(one_minus_a2 > 0.0, one_minus_a2 * lax.rsqrt(one_minus_a2), 0.0)
        u = mult * (gate_x * xc)
        for bi in range(nbatch):
            a_ref[n, bi * pitch:bi * pitch + tt, :] = a[bi * tt:(bi + 1) * tt]
            u_ref[n, bi * pitch:bi * pitch + tt, :] = u[bi * tt:(bi + 1) * tt]

        for g in range(nbatch // half):
            h = h_ref[g * half:(g + 1) * half, cols]
            for t in range(tt):
                step = pl.ds(g * half * pitch + t, half, stride=pitch)
                h = a_ref[n, step, :] * h + u_ref[n, step, :]
                u_ref[n, step, :] = h
            h_ref[g * half:(g + 1) * half, cols] = h

        hs = jnp.concatenate([u_ref[n, bi * pitch:bi * pitch + tt, :] for bi in range(nbatch)], axis=0)
        rec_ref[:, :, cols] = (hs * _gelu_tanh(yg[:, cols])).astype(BF16).reshape(nbatch, tt, bw)

    xext_ref[:, 0:halo, :] = xext_ref[:, tt:tt + halo, :]


def _in_proj_lru(x, w_bf16, qkv_scale, conv_w, conv_b, wgate, ba, bx, lam):
    nbatch, seq, d = x.shape
    nblk, bw, _ = wgate.shape
    lru_w = nblk * bw
    n_in = w_bf16.shape[1]
    n_qkv = n_in - 2 * lru_w
    tt = _tile(seq, 32)
    rows = nbatch * tt
    pitch = tt + V7X_SUBLANES
    chunk = _tile(n_qkv, 1024)
    assert nbatch % V7X_SUBLANES == 0 and tt % BF16_SUBLANES == 0
    est = (d * n_in * 2 + 2 * rows * d * 4 + 2 * rows * (n_qkv + lru_w) * 2 + 3 * nbatch * pitch * lru_w * 4
           + rows * d * 2 + 3 * rows * lru_w * 4 + 2 * rows * chunk * 4)
    fixed = lambda t: (0, 0)
    kern = functools.partial(_in_proj_lru_kernel, chunk=chunk)
    return pl.pallas_call(
        kern,
        grid=(seq // tt,),
        in_specs=[pl.BlockSpec((nbatch, tt, d), lambda t: (0, t, 0)),
                  pl.BlockSpec((d, n_in), fixed, pipeline_mode=pl.Buffered(1)),
                  pl.BlockSpec((1, n_qkv), fixed),
                  pl.BlockSpec((CONV_W, lru_w), fixed),
                  pl.BlockSpec((1, lru_w), fixed),
                  pl.BlockSpec((nblk, bw, 2 * bw), lambda t: (0, 0, 0)),
                  pl.BlockSpec((1, lru_w), fixed),
                  pl.BlockSpec((1, lru_w), fixed),
                  pl.BlockSpec((1, lru_w), fixed)],
        out_specs=[pl.BlockSpec((nbatch, tt, n_qkv), lambda t: (0, t, 0)),
                   pl.BlockSpec((nbatch, tt, lru_w), lambda t: (0, t, 0))],
        out_shape=[jax.ShapeDtypeStruct((nbatch, seq, n_qkv), BF16),
                   jax.ShapeDtypeStruct((nbatch, seq, lru_w), BF16)],
        scratch_shapes=[pltpu.VMEM((nbatch, pitch, lru_w), F32),
                        pltpu.VMEM((nblk, nbatch * pitch, bw), F32),
                        pltpu.VMEM((nblk, nbatch * pitch, bw), F32),
                        pltpu.VMEM((nbatch, lru_w), F32)],
        compiler_params=pltpu.CompilerParams(
            dimension_semantics=("arbitrary",), vmem_limit_bytes=_vmem_limit(est)),
        name="in_proj_rg_lru",
    )(x, w_bf16, qkv_scale, conv_w, conv_b, wgate, ba, bx, lam)


def _bucket_thresholds():
    n = np.arange(0, 8 * REL_MAX_DIST)
    max_exact = REL_BUCKETS // 2
    nf = np.maximum(n, 1).astype(np.float32)
    large = max_exact + (np.log(nf / np.float32(max_exact)) / np.float32(math.log(REL_MAX_DIST / max_exact))
                         * np.float32(REL_BUCKETS - max_exact)).astype(np.int32)
    bucket = np.where(n < max_exact, n, np.minimum(large, REL_BUCKETS - 1))
    assert np.all(np.diff(bucket) >= 0) and bucket[-1] == REL_BUCKETS - 1
    return [int(np.argmax(bucket >= j)) for j in range(1, REL_BUCKETS)]


def _near_key_blocks(tq):
    last = _bucket_thresholds()[-1]
    d = 0
    while d * tq - (tq - 1) < last:
        d += 1
    return d


def _bias_kernel(table_ref, o_ref, *, thresholds):
    h = pl.program_id(0)
    nb, tk, tq = o_ref.shape[1], o_ref.shape[2], o_ref.shape[3]
    key = lax.broadcasted_iota(jnp.int32, (tk, tq), 0)
    qry = lax.broadcasted_iota(jnp.int32, (tk, tq), 1)
    far = table_ref[REL_BUCKETS - 1, h]
    for d in range(nb):
        dist = d * tk + qry - key
        val = jnp.full((tk, tq), table_ref[0, h], F32)
        for j, thr in enumerate(thresholds):
            val = jnp.where(dist >= thr, table_ref[j + 1, h], val)
        o_ref[0, d] = jnp.where(dist < 0, -jnp.inf, (val - far) * LOG2_E)


def _bias_tiles(rel_bias, tq):
    heads = rel_bias.shape[1]
    nb = _near_key_blocks(tq)
    return pl.pallas_call(
        functools.partial(_bias_kernel, thresholds=_bucket_thresholds()),
        grid=(heads,),
        in_specs=[pl.BlockSpec(memory_space=pltpu.SMEM)],
        out_specs=pl.BlockSpec((1, nb, tq, tq), lambda h: (h, 0, 0, 0)),
        out_shape=jax.ShapeDtypeStruct((heads, nb, tq, tq), F32),
        compiler_params=pltpu.CompilerParams(
            dimension_semantics=("parallel",), vmem_limit_bytes=_vmem_limit(2 * nb * tq * tq * 4)),
        name="rel_bias_tiles",
    )(rel_bias)


def _attn_kernel(q_ref, k_ref, v_ref, bias_ref, lqk_ref, g_ref, o_ref, *, tq, lam_init):
    seq = q_ref.shape[1]
    nb = bias_ref.shape[1]
    nt_dims = (((1,), (1,)), ((), ()))

    lqk = lqk_ref[...]
    lam = (jnp.exp(jnp.sum(lqk[0:1] * lqk[1:2], axis=-1, keepdims=True))
           - jnp.exp(jnp.sum(lqk[2:3] * lqk[3:4], axis=-1, keepdims=True)) + lam_init)
    ones_rows = (lax.broadcasted_iota(jnp.int32, (BF16_SUBLANES, seq), 0) == 0).astype(F32)
    v_aug_t = jnp.concatenate([v_ref[0].astype(F32).T, ones_rows], axis=0).astype(BF16)
    lane = lax.broadcasted_iota(jnp.int32, (tq, ATT_DV), 1)

    def scores(i):
        q = q_ref[0, i * tq:(i + 1) * tq, :]
        k = k_ref[0, 0:(i + 1) * tq, :]
        n_far = max(i + 1 - nb, 0)
        out = []
        for c in range(2):
            q_c = jnp.where((lane >= ATT_DQK) if c else (lane < ATT_DQK), q, jnp.zeros_like(q))
            s_t = lax.dot_general(k, q_c, nt_dims, preferred_element_type=F32)
            parts = [s_t[0:n_far * tq]] if n_far else []
            for j in range(n_far, i + 1):
                parts.append(s_t[j * tq:(j + 1) * tq] + bias_ref[0, i - j])
            m = functools.reduce(jnp.maximum, [jnp.max(part, axis=0, keepdims=True) for part in parts])
            out.append((parts, m))
        return out

    n_q = seq // tq
    queue = [scores(i) for i in range(min(ATT_PIPELINE_DEPTH, n_q))]
    for i in range(n_q):
        current = queue.pop(0)
        if i + ATT_PIPELINE_DEPTH < n_q:
            queue.append(scores(i + ATT_PIPELINE_DEPTH))
        n_keys = (i + 1) * tq
        probs = [jnp.concatenate([jnp.exp2(part - m).astype(BF16) for part in parts], axis=0) for parts, m in current]
        pv = _dot(v_aug_t[:, 0:n_keys], jnp.concatenate(probs, axis=1))
        num1, num2 = pv[0:ATT_DV, 0:tq], pv[0:ATT_DV, tq:]
        sum1, sum2 = pv[ATT_DV:ATT_DV + 1, 0:tq], pv[ATT_DV:ATT_DV + 1, tq:]
        out_t = num1 * (1.0 / sum1) - num2 * (lam / sum2)
        rms = lax.rsqrt(jnp.mean(out_t * out_t, axis=0, keepdims=True) + LN_EPS)
        o_ref[0, i * tq:(i + 1) * tq, :] = ((out_t * rms).T * g_ref[...] * (1.0 - lam_init)).astype(o_ref.dtype)


def _diff_attention(qkv, bias_tiles, lqk, subln_g, *, heads, tq, lam_init):
    b, s, _ = qkv.shape
    nb = bias_tiles.shape[1]
    live_q_blocks = 4
    est = 2 * 4 * s * ATT_DV * 2 + 2 * nb * tq * tq * 4 + live_q_blocks * 2 * 2 * s * tq * 4
    kern = functools.partial(_attn_kernel, tq=tq, lam_init=lam_init)
    return pl.pallas_call(
        kern,
        grid=(b, heads),
        in_specs=[pl.BlockSpec((1, s, ATT_DV), lambda bi, h: (bi, 0, h)),
                  pl.BlockSpec((1, s, ATT_DV), lambda bi, h: (bi, 0, heads + h)),
                  pl.BlockSpec((1, s, ATT_DV), lambda bi, h: (bi, 0, 2 * heads + h)),
                  pl.BlockSpec((1, nb, tq, tq), lambda bi, h: (h, 0, 0, 0)),
                  pl.BlockSpec(lqk.shape, lambda bi, h: (0, 0)),
                  pl.BlockSpec((1, ATT_DV), lambda bi, h: (0, 0))],
        out_specs=pl.BlockSpec((1, s, ATT_DV), lambda bi, h: (bi, 0, h)),
        out_shape=jax.ShapeDtypeStruct((b, s, heads * ATT_DV), BF16),
        compiler_params=pltpu.CompilerParams(
            dimension_semantics=("parallel", "parallel"), vmem_limit_bytes=_vmem_limit(est)),
        name="diff_attention",
    )(qkv, qkv, qkv, bias_tiles, lqk, subln_g)


def _out_proj_kernel(x_ref, rec_ref, att_ref, w_ref, g_ref, b_ref, o_ref, ob_ref, *, alpha):
    w1 = rec_ref.shape[1]
    m = _dot(rec_ref[...], w_ref[0:w1, :]) + _dot(att_ref[...], w_ref[w1:, :])
    h = _layer_norm(alpha * x_ref[...] + m, g_ref[...], b_ref[...])
    o_ref[...] = h
    ob_ref[...] = h.astype(BF16)


def _out_proj_ln(x2d, rec2d, att2d, w_bf16, g, b, alpha):
    m, d = x2d.shape
    w1, w2 = rec2d.shape[1], att2d.shape[1]
    tm = _tile(m, 512)
    est = 2 * tm * d * 4 * 2 + 2 * tm * (w1 + w2) * 2 + 2 * (w1 + w2) * d * 2 + 2 * tm * d * 4 + 2 * tm * d * 2
    rowblk = lambda i: (i, 0)
    fixed = lambda i: (0, 0)
    return pl.pallas_call(
        functools.partial(_out_proj_kernel, alpha=alpha),
        grid=(m // tm,),
        in_specs=[pl.BlockSpec((tm, d), rowblk), pl.BlockSpec((tm, w1), rowblk), pl.BlockSpec((tm, w2), rowblk),
                  pl.BlockSpec((w1 + w2, d), fixed), pl.BlockSpec((1, d), fixed), pl.BlockSpec((1, d), fixed)],
        out_specs=[pl.BlockSpec((tm, d), rowblk), pl.BlockSpec((tm, d), rowblk)],
        out_shape=[jax.ShapeDtypeStruct((m, d), F32), jax.ShapeDtypeStruct((m, d), BF16)],
        compiler_params=pltpu.CompilerParams(
            dimension_semantics=("parallel",), vmem_limit_bytes=_vmem_limit(est)),
        name="out_proj_ln1",
    )(x2d, rec2d, att2d, w_bf16, g, b)


def _ffn_kernel(hb_ref, wg_ref, wu_ref, wd_ref, o_ref):
    @pl.when(pl.program_id(1) == 0)
    def _():
        o_ref[...] = jnp.zeros(o_ref.shape, F32)

    hb = hb_ref[...]
    gate = _dot(hb, wg_ref[...])
    up = _dot(hb, wu_ref[...])
    act = (gate * jax.nn.sigmoid(gate)) * up
    o_ref[...] += _dot(act.astype(BF16), wd_ref[...])


def _ffn(hb2d, wg, wu, wd):
    m, d = hb2d.shape
    f = wg.shape[1]
    tm, tf = _tile(m, 1024), _tile(f, 512)
    est = 2 * tm * d * 2 + 2 * 3 * d * tf * 2 + 2 * tm * d * 4 + 3 * tm * tf * 4 + tm * d * 4
    rowblk = lambda i, j: (i, 0)
    return pl.pallas_call(
        _ffn_kernel,
        grid=(m // tm, f // tf),
        in_specs=[pl.BlockSpec((tm, d), rowblk),
                  pl.BlockSpec((d, tf), lambda i, j: (0, j)),
                  pl.BlockSpec((d, tf), lambda i, j: (0, j)),
                  pl.BlockSpec((tf, d), lambda i, j: (j, 0))],
        out_specs=pl.BlockSpec((tm, d), rowblk),
        out_shape=jax.ShapeDtypeStruct((m, d), F32),
        compiler_params=pltpu.CompilerParams(
            dimension_semantics=("parallel", "arbitrary"), vmem_limit_bytes=_vmem_limit(est)),
        name="swiglu",
    )(hb2d, wg, wu, wd)


def _ple_kernel(h_ref, f_ref, p_ref, g2_ref, b2_ref, wg_ref, bg_ref, wp_ref, g3_ref, b3_ref, o_ref, *, alpha):
    h = _layer_norm(alpha * h_ref[...] + f_ref[...], g2_ref[...], b2_ref[...])
    gate = jax.nn.sigmoid(_dot(h.astype(BF16), wg_ref[...]) + bg_ref[...])
    e = _dot(p_ref[...].astype(BF16), wp_ref[...])
    o_ref[...] = _layer_norm(alpha * h + gate * e, g3_ref[...], b3_ref[...])


def _ln2_ple_ln3(h2d, f2d, p2d, g2, b2, wg, bg, wp, g3, b3, alpha):
    m, d = h2d.shape
    pd = p2d.shape[1]
    tm = _tile(m, 512)
    est = 3 * 2 * tm * d * 4 + 2 * tm * pd * 4 + 2 * d * d * 2 + 2 * pd * d * 2 + 3 * tm * d * 4
    rowblk = lambda i: (i, 0)
    fixed = lambda i: (0, 0)
    vec = pl.BlockSpec((1, d), fixed)
    return pl.pallas_call(
        functools.partial(_ple_kernel, alpha=alpha),
        grid=(m // tm,),
        in_specs=[pl.BlockSpec((tm, d), rowblk), pl.BlockSpec((tm, d), rowblk), pl.BlockSpec((tm, pd), rowblk),
                  vec, vec, pl.BlockSpec((d, d), fixed), vec, pl.BlockSpec((pd, d), fixed), vec, vec],
        out_specs=pl.BlockSpec((tm, d), rowblk),
        out_shape=jax.ShapeDtypeStruct((m, d), F32),
        compiler_params=pltpu.CompilerParams(
            dimension_semantics=("parallel",), vmem_limit_bytes=_vmem_limit(est)),
        name="ln2_ple_ln3",
    )(h2d, f2d, p2d, g2, b2, wg, bg, wp, g3, b3)


def kernel(x, p, w_in, conv_w, conv_b, lru_wa, lru_ba, lru_wx, lru_bx, lru_lambda, diff_lq1, diff_lk1, diff_lq2, diff_lk2, diff_subln_g, rel_bias, w_out, ln1_g, ln1_b, w_ffn_gate, w_ffn_up, w_ffn_down, ln2_g, ln2_b, w_ple_gate, b_ple_gate, w_ple_proj, ln3_g, ln3_b):
    batch, seq, d = x.shape
    depth = w_in.shape[0]
    nblk, bw = lru_wa.shape[1], lru_wa.shape[2]
    lru_w = nblk * bw
    heads = rel_bias.shape[1]
    att_qk = 2 * heads * ATT_DQK
    assert w_in.shape[2] == 2 * lru_w + 2 * att_qk + heads * ATT_DV
    assert lru_w % ATT_DV == 0 and att_qk == heads * ATT_DV
    m = batch * seq
    alpha = (2 * depth) ** 0.25

    tq = _tile(seq, 256)
    bias_tiles = _bias_tiles(rel_bias, tq)
    row = lambda v: v.reshape(1, -1)
    col = jnp.arange(2 * att_qk + heads * ATT_DV)
    qkv_scale = row(jnp.where(col < att_qk, ATT_DQK ** -0.5 * LOG2_E, 1.0).astype(F32))

    h = x.reshape(m, d)
    for i in range(depth):
        lam_init = 0.8 - 0.6 * math.exp(-0.3 * i)
        wgate = jnp.concatenate([lru_wa[i], lru_wx[i]], axis=-1).astype(BF16)
        qkv, rec = _in_proj_lru(h.reshape(batch, seq, d), w_in[i].astype(BF16), qkv_scale, conv_w[i], row(conv_b[i]),
                                wgate, row(lru_ba[i]), row(lru_bx[i]), row(lru_lambda[i]))
        lqk = jnp.stack([diff_lq1[i], diff_lk1[i], diff_lq2[i], diff_lk2[i]])
        att = _diff_attention(qkv, bias_tiles, lqk, row(diff_subln_g[i]), heads=heads, tq=tq, lam_init=lam_init)
        h, hb = _out_proj_ln(h, rec.reshape(m, lru_w), att.reshape(m, heads * ATT_DV), w_out[i].astype(BF16),
                             row(ln1_g[i]), row(ln1_b[i]), alpha)
        f = _ffn(hb, w_ffn_gate[i].astype(BF16), w_ffn_up[i].astype(BF16), w_ffn_down[i].astype(BF16))
        h = _ln2_ple_ln3(h, f, p[i].reshape(m, -1), row(ln2_g[i]), row(ln2_b[i]), w_ple_gate[i].astype(BF16),
                         row(b_ple_gate[i]), w_ple_proj[i].astype(BF16), row(ln3_g[i]), row(ln3_b[i]), alpha)
    return h.reshape(batch, seq, d)
```

```python
import functools
import math

import jax
import jax.numpy as jnp
import numpy as np
from jax import lax
from jax.experimental import pallas as pl
from jax.experimental.pallas import tpu as pltpu

F32 = jnp.float32
BF16 = jnp.bfloat16

LN_EPS = 1e-5
LRU_C = 8.0
CONV_W = 4
ATT_DV = 128
ATT_DQK = ATT_DV // 2
REL_BUCKETS = 32
REL_MAX_DIST = 128
ATT_PIPELINE_DEPTH = 2
LOG2_E = math.log2(math.e)

V7X_LANES = 128
V7X_SUBLANES = 8
BF16_SUBLANES = 2 * V7X_SUBLANES
V7X_VMEM_BYTES = 64 * 1024 * 1024
V7X_VMEM_RESERVE_BYTES = 6 * 1024 * 1024


def _vmem_limit(estimate_bytes):
    return int(min(V7X_VMEM_BYTES - V7X_VMEM_RESERVE_BYTES, max(2 * estimate_bytes, 16 * 1024 * 1024)))


def _tile(dim, preferred):
    if dim % preferred == 0:
        return preferred
    return dim


def _layer_norm(y, g, b):
    mu = jnp.mean(y, axis=-1, keepdims=True)
    yc = y - mu
    var = jnp.mean(yc * yc, axis=-1, keepdims=True)
    return yc * lax.rsqrt(var + LN_EPS) * g + b


def _dot(a, b):
    return jnp.dot(a, b, preferred_element_type=F32)


def _gelu_tanh(x):
    return 0.5 * x * (1.0 + jnp.tanh(math.sqrt(2.0 / math.pi) * (x + 0.044715 * (x * x * x))))


def _in_proj_lru_kernel(x_ref, w_ref, cs_ref, cw_ref, cb_ref, wg_ref, ba_ref, bx_ref, lam_ref,
                        qkv_ref, rec_ref, xext_ref, a_ref, u_ref, h_ref, *, chunk):
    nbatch, tt, d = x_ref.shape
    rows = nbatch * tt
    lru_w = rec_ref.shape[2]
    n_qkv = qkv_ref.shape[2]
    bw = wg_ref.shape[1]
    halo = V7X_SUBLANES
    pitch = xext_ref.shape[1]
    half = V7X_SUBLANES

    @pl.when(pl.program_id(0) == 0)
    def _():
        xext_ref[:, 0:halo, :] = jnp.zeros((nbatch, halo, lru_w), F32)
        h_ref[...] = jnp.zeros(h_ref.shape, F32)

    xb = x_ref[...].reshape(rows, d).astype(BF16)
    xext_ref[:, halo:halo + tt, :] = _dot(xb, w_ref[:, 0:lru_w]).reshape(nbatch, tt, lru_w)
    yg = _dot(xb, w_ref[:, lru_w:2 * lru_w])
    for c0 in range(0, n_qkv, chunk):
        val = _dot(xb, w_ref[:, 2 * lru_w + c0:2 * lru_w + c0 + chunk]) * cs_ref[:, c0:c0 + chunk]
        qkv_ref[:, :, c0:c0 + chunk] = val.astype(BF16).reshape(nbatch, tt, chunk)

    for n in range(lru_w // bw):
        cols = slice(n * bw, (n + 1) * bw)
        xc = cb_ref[:, cols]
        for t in range(CONV_W):
            start = halo - (CONV_W - 1) + t
            xc = xc + xext_ref[:, start:start + tt, cols].reshape(rows, bw) * cw_ref[t:t + 1, cols]
        gates = _dot(xc.astype(BF16), wg_ref[n])
        gate_a = jax.nn.sigmoid(gates[:, :bw] + ba_ref[:, cols])
        gate_x = jax.nn.sigmoid(gates[:, bw:] + bx_ref[:, cols])
        neg_lam = -lam_ref[:, cols]
        softplus = jnp.maximum(neg_lam, 0.0) + jnp.log1p(jnp.exp(-jnp.abs(neg_lam)))
        log_a = -LRU_C * gate_a * softplus
        a = jnp.exp(log_a)
        one_minus_a2 = -jnp.tanh(log_a) * (a * a + 1.0)
        mult = jnp.where(one_minus_a2 > 0.0, one_minus_a2 * lax.rsqrt(one_minus_a2), 0.0)
        u = mult * (gate_x * xc)
        for bi in range(nbatch):
            a_ref[n, bi * pitch:bi * pitch + tt, :] = a[bi * tt:(bi + 1) * tt]
            u_ref[n, bi * pitch:bi * pitch + tt, :] = u[bi * tt:(bi + 1) * tt]

        for g in range(nbatch // half):
            h = h_ref[g * half:(g + 1) * half, cols]
            for t in range(tt):
                step = pl.ds(g * half * pitch + t, half, stride=pitch)
                h = a_ref[n, step, :] * h + u_ref[n, step, :]
                u_ref[n, step, :] = h
            h_ref[g * half:(g + 1) * half, cols] = h

        hs = jnp.concatenate([u_ref[n, bi * pitch:bi * pitch + tt, :] for bi in range(nbatch)], axis=0)
        rec_ref[:, :, cols] = (hs * _gelu_tanh(yg[:, cols])).astype(BF16).reshape(nbatch, tt, bw)

    xext_ref[:, 0:halo, :] = xext_ref[:, tt:tt + halo, :]


def _in_proj_lru(x, w_bf16, qkv_scale, conv_w, conv_b, wgate, ba, bx, lam):
    nbatch, seq, d = x.shape
    nblk, bw, _ = wgate.shape
    lru_w = nblk * bw
    n_in = w_bf16.shape[1]
    n_qkv = n_in - 2 * lru_w
    tt = _tile(seq, 32)
    rows = nbatch * tt
    pitch = tt + V7X_SUBLANES
    chunk = _tile(n_qkv, 1024)
    assert nbatch % V7X_SUBLANES == 0 and tt % BF16_SUBLANES == 0
    est = (d * n_in * 2 + 2 * rows * d * 4 + 2 * rows * (n_qkv + lru_w) * 2 + 3 * nbatch * pitch * lru_w * 4
           + rows * d * 2 + 3 * rows * lru_w * 4 + 2 * rows * chunk * 4)
    fixed = lambda t: (0, 0)
    kern = functools.partial(_in_proj_lru_kernel, chunk=chunk)
    return pl.pallas_call(
        kern,
        grid=(seq // tt,),
        in_specs=[pl.BlockSpec((nbatch, tt, d), lambda t: (0, t, 0)),
                  pl.BlockSpec((d, n_in), fixed, pipeline_mode=pl.Buffered(1)),
                  pl.BlockSpec((1, n_qkv), fixed),
                  pl.BlockSpec((CONV_W, lru_w), fixed),
                  pl.BlockSpec((1, lru_w), fixed),
                  pl.BlockSpec((nblk, bw, 2 * bw), lambda t: (0, 0, 0)),
                  pl.BlockSpec((1, lru_w), fixed),
                  pl.BlockSpec((1, lru_w), fixed),
                  pl.BlockSpec((1, lru_w), fixed)],
        out_specs=[pl.BlockSpec((nbatch, tt, n_qkv), lambda t: (0, t, 0)),
                   pl.BlockSpec((nbatch, tt, lru_w), lambda t: (0, t, 0))],
        out_shape=[jax.ShapeDtypeStruct((nbatch, seq, n_qkv), BF16),
                   jax.ShapeDtypeStruct((nbatch, seq, lru_w), BF16)],
        scratch_shapes=[pltpu.VMEM((nbatch, pitch, lru_w), F32),
                        pltpu.VMEM((nblk, nbatch * pitch, bw), F32),
                        pltpu.VMEM((nblk, nbatch * pitch, bw), F32),
                        pltpu.VMEM((nbatch, lru_w), F32)],
        compiler_params=pltpu.CompilerParams(
            dimension_semantics=("arbitrary",), vmem_limit_bytes=_vmem_limit(est)),
        name="in_proj_rg_lru",
    )(x, w_bf16, qkv_scale, conv_w, conv_b, wgate, ba, bx, lam)


def _bucket_thresholds():
    n = np.arange(0, 8 * REL_MAX_DIST)
    max_exact = REL_BUCKETS // 2
    nf = np.maximum(n, 1).astype(np.float32)
    large = max_exact + (np.log(nf / np.float32(max_exact)) / np.float32(math.log(REL_MAX_DIST / max_exact))
                         * np.float32(REL_BUCKETS - max_exact)).astype(np.int32)
    bucket = np.where(n < max_exact, n, np.minimum(large, REL_BUCKETS - 1))
    assert np.all(np.diff(bucket) >= 0) and bucket[-1] == REL_BUCKETS - 1
    return [int(np.argmax(bucket >= j)) for j in range(1, REL_BUCKETS)]


def _near_key_blocks(tq):
    last = _bucket_thresholds()[-1]
    d = 0
    while d * tq - (tq - 1) < last:
        d += 1
    return d


def _bias_kernel(table_ref, o_ref, *, thresholds):
    h = pl.program_id(0)
    nb, tk, tq = o_ref.shape[1], o_ref.shape[2], o_ref.shape[3]
    key = lax.broadcasted_iota(jnp.int32, (tk, tq), 0)
    qry = lax.broadcasted_iota(jnp.int32, (tk, tq), 1)
    far = table_ref[REL_BUCKETS - 1, h]
    for d in range(nb):
        dist = d * tk + qry - key
        val = jnp.full((tk, tq), table_ref[0, h], F32)
        for j, thr in enumerate(thresholds):
            val = jnp.where(dist >= thr, table_ref[j + 1, h], val)
        o_ref[0, d] = jnp.where(dist < 0, -jnp.inf, (val - far) * LOG2_E)


def _bias_tiles(rel_bias, tq):
    heads = rel_bias.shape[1]
    nb = _near_key_blocks(tq)
    return pl.pallas_call(
        functools.partial(_bias_kernel, thresholds=_bucket_thresholds()),
        grid=(heads,),
        in_specs=[pl.BlockSpec(memory_space=pltpu.SMEM)],
        out_specs=pl.BlockSpec((1, nb, tq, tq), lambda h: (h, 0, 0, 0)),
        out_shape=jax.ShapeDtypeStruct((heads, nb, tq, tq), F32),
        compiler_params=pltpu.CompilerParams(
            dimension_semantics=("parallel",), vmem_limit_bytes=_vmem_limit(2 * nb * tq * tq * 4)),
        name="rel_bias_tiles",
    )(rel_bias)


def _attn_kernel(q_ref, k_ref, v_ref, bias_ref, lqk_ref, g_ref, o_ref, *, tq, lam_init):
    seq = q_ref.shape[1]
    nb = bias_ref.shape[1]
    nt_dims = (((1,), (1,)), ((), ()))

    lqk = lqk_ref[...]
    lam = (jnp.exp(jnp.sum(lqk[0:1] * lqk[1:2], axis=-1, keepdims=True))
           - jnp.exp(jnp.sum(lqk[2:3] * lqk[3:4], axis=-1, keepdims=True)) + lam_init)
    ones_rows = (lax.broadcasted_iota(jnp.int32, (BF16_SUBLANES, seq), 0) == 0).astype(F32)
    v_aug_t = jnp.concatenate([v_ref[0].astype(F32).T, ones_rows], axis=0).astype(BF16)
    lane = lax.broadcasted_iota(jnp.int32, (tq, ATT_DV), 1)

    def scores(i):
        q = q_ref[0, i * tq:(i + 1) * tq, :]
        k = k_ref[0, 0:(i + 1) * tq, :]
        n_far = max(i + 1 - nb, 0)
        out = []
        for c in range(2):
            q_c = jnp.where((lane >= ATT_DQK) if c else (lane < ATT_DQK), q, jnp.zeros_like(q))
            s_t = lax.dot_general(k, q_c, nt_dims, preferred_element_type=F32)
            parts = [s_t[0:n_far * tq]] if n_far else []
            for j in range(n_far, i + 1):
                parts.append(s_t[j * tq:(j + 1) * tq] + bias_ref[0, i - j])
            m = functools.reduce(jnp.maximum, [jnp.max(part, axis=0, keepdims=True) for part in parts])
            out.append((parts, m))
        return out

    n_q = seq // tq
    queue = [scores(i) for i in range(min(ATT_PIPELINE_DEPTH, n_q))]
    for i in range(n_q):
        current = queue.pop(0)
        if i + ATT_PIPELINE_DEPTH < n_q:
            queue.append(scores(i + ATT_PIPELINE_DEPTH))
        n_keys = (i + 1) * tq
        probs = [jnp.concatenate([jnp.exp2(part - m).astype(BF16) for part in parts], axis=0) for parts, m in current]
        pv = _dot(v_aug_t[:, 0:n_keys], jnp.concatenate(probs, axis=1))
        num1, num2 = pv[0:ATT_DV, 0:tq], pv[0:ATT_DV, tq:]
        sum1, sum2 = pv[ATT_DV:ATT_DV + 1, 0:tq], pv[ATT_DV:ATT_DV + 1, tq:]
        out_t = num1 * (1.0 / sum1) - num2 * (lam / sum2)
        rms = lax.rsqrt(jnp.mean(out_t * out_t, axis=0, keepdims=True) + LN_EPS)
        o_ref[0, i * tq:(i + 1) * tq, :] = ((out_t * rms).T * g_ref[...] * (1.0 - lam_init)).astype(o_ref.dtype)


def _diff_attention(qkv, bias_tiles, lqk, subln_g, *, heads, tq, lam_init):
    b, s, _ = qkv.shape
    nb = bias_tiles.shape[1]
    live_q_blocks = 4
    est = 2 * 4 * s * ATT_DV * 2 + 2 * nb * tq * tq * 4 + live_q_blocks * 2 * 2 * s * tq * 4
    kern = functools.partial(_attn_kernel, tq=tq, lam_init=lam_init)
    return pl.pallas_call(
        kern,
        grid=(b, heads),
        in_specs=[pl.BlockSpec((1, s, ATT_DV), lambda bi, h: (bi, 0, h)),
                  pl.BlockSpec((1, s, ATT_DV), lambda bi, h: (bi, 0, heads + h)),
                  pl.BlockSpec((1, s, ATT_DV), lambda bi, h: (bi, 0, 2 * heads + h)),
                  pl.BlockSpec((1, nb, tq, tq), lambda bi, h: (h, 0, 0, 0)),
                  pl.BlockSpec(lqk.shape, lambda bi, h: (0, 0)),
                  pl.BlockSpec((1, ATT_DV), lambda bi, h: (0, 0))],
        out_specs=pl.BlockSpec((1, s, ATT_DV), lambda bi, h: (bi, 0, h)),
        out_shape=jax.ShapeDtypeStruct((b, s, heads * ATT_DV), BF16),
        compiler_params=pltpu.CompilerParams(
            dimension_semantics=("parallel", "parallel"), vmem_limit_bytes=_vmem_limit(est)),
        name="diff_attention",
    )(qkv, qkv, qkv, bias_tiles, lqk, subln_g)


def _out_proj_kernel(x_ref, rec_ref, att_ref, w_ref, g_ref, b_ref, o_ref, ob_ref, *, alpha):
    w1 = rec_ref.shape[1]
    m = _dot(rec_ref[...], w_ref[0:w1, :]) + _dot(att_ref[...], w_ref[w1:, :])
    h = _layer_norm(alpha * x_ref[...] + m, g_ref[...], b_ref[...])
    o_ref[...] = h
    ob_ref[...] = h.astype(BF16)


def _out_proj_ln(x2d, rec2d, att2d, w_bf16, g, b, alpha):
    m, d = x2d.shape
    w1, w2 = rec2d.shape[1], att2d.shape[1]
    tm = _tile(m, 512)
    est = 2 * tm * d * 4 * 2 + 2 * tm * (w1 + w2) * 2 + 2 * (w1 + w2) * d * 2 + 2 * tm * d * 4 + 2 * tm * d * 2
    rowblk = lambda i: (i, 0)
    fixed = lambda i: (0, 0)
    return pl.pallas_call(
        functools.partial(_out_proj_kernel, alpha=alpha),
        grid=(m // tm,),
        in_specs=[pl.BlockSpec((tm, d), rowblk), pl.BlockSpec((tm, w1), rowblk), pl.BlockSpec((tm, w2), rowblk),
                  pl.BlockSpec((w1 + w2, d), fixed), pl.BlockSpec((1, d), fixed), pl.BlockSpec((1, d), fixed)],
        out_specs=[pl.BlockSpec((tm, d), rowblk), pl.BlockSpec((tm, d), rowblk)],
        out_shape=[jax.ShapeDtypeStruct((m, d), F32), jax.ShapeDtypeStruct((m, d), BF16)],
        compiler_params=pltpu.CompilerParams(
            dimension_semantics=("parallel",), vmem_limit_bytes=_vmem_limit(est)),
        name="out_proj_ln1",
    )(x2d, rec2d, att2d, w_bf16, g, b)


def _ffn_kernel(hb_ref, wg_ref, wu_ref, wd_ref, o_ref):
    @pl.when(pl.program_id(1) == 0)
    def _():
        o_ref[...] = jnp.zeros(o_ref.shape, F32)

    hb = hb_ref[...]
    gate = _dot(hb, wg_ref[...].astype(BF16))
    up = _dot(hb, wu_ref[...].astype(BF16))
    act = (gate * jax.nn.sigmoid(gate)) * up
    o_ref[...] += _dot(act.astype(BF16), wd_ref[...].astype(BF16))


def _ffn(hb2d, wg, wu, wd):
    m, d = hb2d.shape
    f = wg.shape[1]
    tm, tf = _tile(m, 1024), _tile(f, 512)
    est = 2 * tm * d * 2 + 2 * 3 * d * tf * 4 + 2 * tm * d * 4 + 3 * tm * tf * 4 + tm * d * 4
    rowblk = lambda i, j: (i, 0)
    return pl.pallas_call(
        _ffn_kernel,
        grid=(m // tm, f // tf),
        in_specs=[pl.BlockSpec((tm, d), rowblk),
                  pl.BlockSpec((d, tf), lambda i, j: (0, j)),
                  pl.BlockSpec((d, tf), lambda i, j: (0, j)),
                  pl.BlockSpec((tf, d), lambda i, j: (j, 0))],
        out_specs=pl.BlockSpec((tm, d), rowblk),
        out_shape=jax.ShapeDtypeStruct((m, d), F32),
        compiler_params=pltpu.CompilerParams(
            dimension_semantics=("parallel", "arbitrary"), vmem_limit_bytes=_vmem_limit(est)),
        name="swiglu",
    )(hb2d, wg, wu, wd)


def _ple_kernel(h_ref, f_ref, p_ref, g2_ref, b2_ref, wg_ref, bg_ref, wp_ref, g3_ref, b3_ref, o_ref, *, alpha):
    h = _layer_norm(alpha * h_ref[...] + f_ref[...], g2_ref[...], b2_ref[...])
    gate = jax.nn.sigmoid(_dot(h.astype(BF16), wg_ref[...]) + bg_ref[...])
    e = _dot(p_ref[...].astype(BF16), wp_ref[...])
    o_ref[...] = _layer_norm(alpha * h + gate * e, g3_ref[...], b3_ref[...])


def _ln2_ple_ln3(h2d, f2d, p2d, g2, b2, wg, bg, wp, g3, b3, alpha):
    m, d = h2d.shape
    pd = p2d.shape[1]
    tm = _tile(m, 512)
    est = 3 * 2 * tm * d * 4 + 2 * tm * pd * 4 + 2 * d * d * 2 + 2 * pd * d * 2 + 3 * tm * d * 4
    rowblk = lambda i: (i, 0)
    fixed = lambda i: (0, 0)
    vec = pl.BlockSpec((1, d), fixed)
    return pl.pallas_call(
        functools.partial(_ple_kernel, alpha=alpha),
        grid=(m // tm,),
        in_specs=[pl.BlockSpec((tm, d), rowblk), pl.BlockSpec((tm, d), rowblk), pl.BlockSpec((tm, pd), rowblk),
                  vec, vec, pl.BlockSpec((d, d), fixed), vec, pl.BlockSpec((pd, d), fixed), vec, vec],
        out_specs=pl.BlockSpec((tm, d), rowblk),
        out_shape=jax.ShapeDtypeStruct((m, d), F32),
        compiler_params=pltpu.CompilerParams(
            dimension_semantics=("parallel",), vmem_limit_bytes=_vmem_limit(est)),
        name="ln2_ple_ln3",
    )(h2d, f2d, p2d, g2, b2, wg, bg, wp, g3, b3)


def kernel(x, p, w_in, conv_w, conv_b, lru_wa, lru_ba, lru_wx, lru_bx, lru_lambda, diff_lq1, diff_lk1, diff_lq2, diff_lk2, diff_subln_g, rel_bias, w_out, ln1_g, ln1_b, w_ffn_gate, w_ffn_up, w_ffn_down, ln2_g, ln2_b, w_ple_gate, b_ple_gate, w_ple_proj, ln3_g, ln3_b):
    batch, seq, d = x.shape
    depth = w_in.shape[0]
    nblk, bw = lru_wa.shape[1], lru_wa.shape[2]
    lru_w = nblk * bw
    heads = rel_bias.shape[1]
    att_qk = 2 * heads * ATT_DQK
    assert w_in.shape[2] == 2 * lru_w + 2 * att_qk + heads * ATT_DV
    assert lru_w % ATT_DV == 0 and att_qk == heads * ATT_DV
    m = batch * seq
    alpha = (2 * depth) ** 0.25

    tq = _tile(seq, 256)
    bias_tiles = _bias_tiles(rel_bias, tq)
    row = lambda v: v.reshape(1, -1)
    col = jnp.arange(2 * att_qk + heads * ATT_DV)
    qkv_scale = row(jnp.where(col < att_qk, ATT_DQK ** -0.5 * LOG2_E, 1.0).astype(F32))

    h = x.reshape(m, d)
    for i in range(depth):
        lam_init = 0.8 - 0.6 * math.exp(-0.3 * i)
        wgate = jnp.concatenate([lru_wa[i], lru_wx[i]], axis=-1).astype(BF16)
        qkv, rec = _in_proj_lru(h.reshape(batch, seq, d), w_in[i].astype(BF16), qkv_scale, conv_w[i], row(conv_b[i]),
                                wgate, row(lru_ba[i]), row(lru_bx[i]), row(lru_lambda[i]))
        lqk = jnp.stack([diff_lq1[i], diff_lk1[i], diff_lq2[i], diff_lk2[i]])
        att = _diff_attention(qkv, bias_tiles, lqk, row(diff_subln_g[i]), heads=heads, tq=tq, lam_init=lam_init)
        h, hb = _out_proj_ln(h, rec.reshape(m, lru_w), att.reshape(m, heads * ATT_DV), w_out[i].astype(BF16),
                             row(ln1_g[i]), row(ln1_b[i]), alpha)
        f = _ffn(hb, w_ffn_gate[i], w_ffn_up[i], w_ffn_down[i])
        h = _ln2_ple_ln3(h, f, p[i].reshape(m, -1), row(ln2_g[i]), row(ln2_b[i]), w_ple_gate[i].astype(BF16),
                         row(b_ple_gate[i]), w_ple_proj[i].astype(BF16), row(ln3_g[i]), row(ln3_b[i]), alpha)
    return h.reshape(batch, seq, d)
```

```python
import functools
import math

import jax
import jax.numpy as jnp
import numpy as np
from jax import lax
from jax.experimental import pallas as pl
from jax.experimental.pallas import tpu as pltpu

F32 = jnp.float32
BF16 = jnp.bfloat16

LN_EPS = 1e-5
LRU_C = 8.0
CONV_W = 4
ATT_DV = 128
ATT_DQK = ATT_DV // 2
REL_BUCKETS = 32
REL_MAX_DIST = 128
ATT_PIPELINE_DEPTH = 2
LOG2_E = math.log2(math.e)

V7X_LANES = 128
V7X_SUBLANES = 8
BF16_SUBLANES = 2 * V7X_SUBLANES
V7X_VMEM_BYTES = 64 * 1024 * 1024
V7X_VMEM_RESERVE_BYTES = 6 * 1024 * 1024


def _vmem_limit(estimate_bytes):
    return int(min(V7X_VMEM_BYTES - V7X_VMEM_RESERVE_BYTES, max(2 * estimate_bytes, 16 * 1024 * 1024)))


def _tile(dim, preferred):
    if dim % preferred == 0:
        return preferred
    return dim


def _layer_norm(y, g, b):
    mu = jnp.mean(y, axis=-1, keepdims=True)
    yc = y - mu
    var = jnp.mean(yc * yc, axis=-1, keepdims=True)
    return yc * lax.rsqrt(var + LN_EPS) * g + b


def _dot(a, b):
    return jnp.dot(a, b, preferred_element_type=F32)


def _gelu_tanh(x):
    return 0.5 * x * (1.0 + jnp.tanh(math.sqrt(2.0 / math.pi) * (x + 0.044715 * (x * x * x))))


def _sigmoid_abs(x):
    return 0.5 * jnp.tanh(0.5 * x) + 0.5


def _in_proj_lru_kernel(x_ref, w_ref, cs_ref, cw_ref, cb_ref, wg_ref, ba_ref, bx_ref, lam_ref,
                        qkv_ref, rec_ref, xext_ref, a_ref, u_ref, h_ref, *, chunk):
    nbatch, tt, d = x_ref.shape
    rows = nbatch * tt
    lru_w = rec_ref.shape[2]
    n_qkv = qkv_ref.shape[2]
    bw = wg_ref.shape[1]
    halo = V7X_SUBLANES
    pitch = xext_ref.shape[1]
    half = V7X_SUBLANES

    @pl.when(pl.program_id(0) == 0)
    def _():
        xext_ref[:, 0:halo, :] = jnp.zeros((nbatch, halo, lru_w), F32)
        h_ref[...] = jnp.zeros(h_ref.shape, F32)

    xb = x_ref[...].reshape(rows, d).astype(BF16)
    xext_ref[:, halo:halo + tt, :] = _dot(xb, w_ref[:, 0:lru_w]).reshape(nbatch, tt, lru_w)
    yg = _dot(xb, w_ref[:, lru_w:2 * lru_w])
    for c0 in range(0, n_qkv, chunk):
        val = _dot(xb, w_ref[:, 2 * lru_w + c0:2 * lru_w + c0 + chunk]) * cs_ref[:, c0:c0 + chunk]
        qkv_ref[:, :, c0:c0 + chunk] = val.astype(BF16).reshape(nbatch, tt, chunk)

    for n in range(lru_w // bw):
        cols = slice(n * bw, (n + 1) * bw)
        xc = cb_ref[:, cols]
        for t in range(CONV_W):
            start = halo - (CONV_W - 1) + t
            xc = xc + xext_ref[:, start:start + tt, cols].reshape(rows, bw) * cw_ref[t:t + 1, cols]
        gates = _dot(xc.astype(BF16), wg_ref[n])
        gate_a = jax.nn.sigmoid(gates[:, :bw] + ba_ref[:, cols])
        gate_x = _sigmoid_abs(gates[:, bw:] + bx_ref[:, cols])
        neg_lam = -lam_ref[:, cols]
        softplus = jnp.maximum(neg_lam, 0.0) + jnp.log1p(jnp.exp(-jnp.abs(neg_lam)))
        log_a = -LRU_C * gate_a * softplus
        a = jnp.exp(log_a)
        one_minus_a2 = -jnp.tanh(log_a) * (a * a + 1.0)
        mult = jnp.where(one_minus_a2 > 0.0, one_minus_a2 * lax.rsqrt(one_minus_a2), 0.0)
        u = mult * (gate_x * xc)
        for bi in range(nbatch):
            a_ref[n, bi * pitch:bi * pitch + tt, :] = a[bi * tt:(bi + 1) * tt]
            u_ref[n, bi * pitch:bi * pitch + tt, :] = u[bi * tt:(bi + 1) * tt]

        for g in range(nbatch // half):
            h = h_ref[g * half:(g + 1) * half, cols]
            for t in range(tt):
                step = pl.ds(g * half * pitch + t, half, stride=pitch)
                h = a_ref[n, step, :] * h + u_ref[n, step, :]
                u_ref[n, step, :] = h
            h_ref[g * half:(g + 1) * half, cols] = h

        hs = jnp.concatenate([u_ref[n, bi * pitch:bi * pitch + tt, :] for bi in range(nbatch)], axis=0)
        rec_ref[:, :, cols] = (hs * _gelu_tanh(yg[:, cols])).astype(BF16).reshape(nbatch, tt, bw)

    xext_ref[:, 0:halo, :] = xext_ref[:, tt:tt + halo, :]


def _in_proj_lru(x, w_bf16, qkv_scale, conv_w, conv_b, wgate, ba, bx, lam):
    nbatch, seq, d = x.shape
    nblk, bw, _ = wgate.shape
    lru_w = nblk * bw
    n_in = w_bf16.shape[1]
    n_qkv = n_in - 2 * lru_w
    tt = _tile(seq, 32)
    rows = nbatch * tt
    pitch = tt + V7X_SUBLANES
    chunk = _tile(n_qkv, 1024)
    assert nbatch % V7X_SUBLANES == 0 and tt % BF16_SUBLANES == 0
    est = (d * n_in * 2 + 2 * rows * d * 4 + 2 * rows * (n_qkv + lru_w) * 2 + 3 * nbatch * pitch * lru_w * 4
           + rows * d * 2 + 3 * rows * lru_w * 4 + 2 * rows * chunk * 4)
    fixed = lambda t: (0, 0)
    kern = functools.partial(_in_proj_lru_kernel, chunk=chunk)
    return pl.pallas_call(
        kern,
        grid=(seq // tt,),
        in_specs=[pl.BlockSpec((nbatch, tt, d), lambda t: (0, t, 0)),
                  pl.BlockSpec((d, n_in), fixed, pipeline_mode=pl.Buffered(1)),
                  pl.BlockSpec((1, n_qkv), fixed),
                  pl.BlockSpec((CONV_W, lru_w), fixed),
                  pl.BlockSpec((1, lru_w), fixed),
                  pl.BlockSpec((nblk, bw, 2 * bw), lambda t: (0, 0, 0)),
                  pl.BlockSpec((1, lru_w), fixed),
                  pl.BlockSpec((1, lru_w), fixed),
                  pl.BlockSpec((1, lru_w), fixed)],
        out_specs=[pl.BlockSpec((nbatch, tt, n_qkv), lambda t: (0, t, 0)),
                   pl.BlockSpec((nbatch, tt, lru_w), lambda t: (0, t, 0))],
        out_shape=[jax.ShapeDtypeStruct((nbatch, seq, n_qkv), BF16),
                   jax.ShapeDtypeStruct((nbatch, seq, lru_w), BF16)],
        scratch_shapes=[pltpu.VMEM((nbatch, pitch, lru_w), F32),
                        pltpu.VMEM((nblk, nbatch * pitch, bw), F32),
                        pltpu.VMEM((nblk, nbatch * pitch, bw), F32),
                        pltpu.VMEM((nbatch, lru_w), F32)],
        compiler_params=pltpu.CompilerParams(
            dimension_semantics=("arbitrary",), vmem_limit_bytes=_vmem_limit(est)),
        name="in_proj_rg_lru",
    )(x, w_bf16, qkv_scale, conv_w, conv_b, wgate, ba, bx, lam)


def _bucket_thresholds():
    n = np.arange(0, 8 * REL_MAX_DIST)
    max_exact = REL_BUCKETS // 2
    nf = np.maximum(n, 1).astype(np.float32)
    large = max_exact + (np.log(nf / np.float32(max_exact)) / np.float32(math.log(REL_MAX_DIST / max_exact))
                         * np.float32(REL_BUCKETS - max_exact)).astype(np.int32)
    bucket = np.where(n < max_exact, n, np.minimum(large, REL_BUCKETS - 1))
    assert np.all(np.diff(bucket) >= 0) and bucket[-1] == REL_BUCKETS - 1
    return [int(np.argmax(bucket >= j)) for j in range(1, REL_BUCKETS)]


def _near_key_blocks(tq):
    last = _bucket_thresholds()[-1]
    d = 0
    while d * tq - (tq - 1) < last:
        d += 1
    return d


def _bias_kernel(table_ref, o_ref, *, thresholds):
    h = pl.program_id(0)
    nb, tk, tq = o_ref.shape[1], o_ref.shape[2], o_ref.shape[3]
    key = lax.broadcasted_iota(jnp.int32, (tk, tq), 0)
    qry = lax.broadcasted_iota(jnp.int32, (tk, tq), 1)
    far = table_ref[REL_BUCKETS - 1, h]
    for d in range(nb):
        dist = d * tk + qry - key
        val = jnp.full((tk, tq), table_ref[0, h], F32)
        for j, thr in enumerate(thresholds):
            val = jnp.where(dist >= thr, table_ref[j + 1, h], val)
        o_ref[0, d] = jnp.where(dist < 0, -jnp.inf, (val - far) * LOG2_E)


def _bias_tiles(rel_bias, tq):
    heads = rel_bias.shape[1]
    nb = _near_key_blocks(tq)
    return pl.pallas_call(
        functools.partial(_bias_kernel, thresholds=_bucket_thresholds()),
        grid=(heads,),
        in_specs=[pl.BlockSpec(memory_space=pltpu.SMEM)],
        out_specs=pl.BlockSpec((1, nb, tq, tq), lambda h: (h, 0, 0, 0)),
        out_shape=jax.ShapeDtypeStruct((heads, nb, tq, tq), F32),
        compiler_params=pltpu.CompilerParams(
            dimension_semantics=("parallel",), vmem_limit_bytes=_vmem_limit(2 * nb * tq * tq * 4)),
        name="rel_bias_tiles",
    )(rel_bias)


def _attn_kernel(q_ref, k_ref, v_ref, bias_ref, lqk_ref, g_ref, o_ref, *, tq, lam_init):
    seq = q_ref.shape[1]
    nb = bias_ref.shape[1]
    nt_dims = (((1,), (1,)), ((), ()))

    lqk = lqk_ref[...]
    lam = (jnp.exp(jnp.sum(lqk[0:1] * lqk[1:2], axis=-1, keepdims=True))
           - jnp.exp(jnp.sum(lqk[2:3] * lqk[3:4], axis=-1, keepdims=True)) + lam_init)
    ones_rows = (lax.broadcasted_iota(jnp.int32, (BF16_SUBLANES, seq), 0) == 0).astype(F32)
    v_aug_t = jnp.concatenate([v_ref[0].astype(F32).T, ones_rows], axis=0).astype(BF16)
    lane = lax.broadcasted_iota(jnp.int32, (tq, ATT_DV), 1)

    def scores(i):
        q = q_ref[0, i * tq:(i + 1) * tq, :]
        k = k_ref[0, 0:(i + 1) * tq, :]
        n_far = max(i + 1 - nb, 0)
        out = []
        for c in range(2):
            q_c = jnp.where((lane >= ATT_DQK) if c else (lane < ATT_DQK), q, jnp.zeros_like(q))
            s_t = lax.dot_general(k, q_c, nt_dims, preferred_element_type=F32)
            parts = [s_t[0:n_far * tq]] if n_far else []
            for j in range(n_far, i + 1):
                parts.append(s_t[j * tq:(j + 1) * tq] + bias_ref[0, i - j])
            m = functools.reduce(jnp.maximum, [jnp.max(part, axis=0, keepdims=True) for part in parts])
            out.append((parts, m))
        return out

    n_q = seq // tq
    queue = [scores(i) for i in range(min(ATT_PIPELINE_DEPTH, n_q))]
    for i in range(n_q):
        current = queue.pop(0)
        if i + ATT_PIPELINE_DEPTH < n_q:
            queue.append(scores(i + ATT_PIPELINE_DEPTH))
        n_keys = (i + 1) * tq
        probs = [jnp.concatenate([jnp.exp2(part - m).astype(BF16) for part in parts], axis=0) for parts, m in current]
        pv = _dot(v_aug_t[:, 0:n_keys], jnp.concatenate(probs, axis=1))
        num1, num2 = pv[0:ATT_DV, 0:tq], pv[0:ATT_DV, tq:]
        sum1, sum2 = pv[ATT_DV:ATT_DV + 1, 0:tq], pv[ATT_DV:ATT_DV + 1, tq:]
        out_t = num1 * (1.0 / sum1) - num2 * (lam / sum2)
        rms = lax.rsqrt(jnp.mean(out_t * out_t, axis=0, keepdims=True) + LN_EPS)
        o_ref[0, i * tq:(i + 1) * tq, :] = ((out_t * rms).T * g_ref[...] * (1.0 - lam_init)).astype(o_ref.dtype)


def _diff_attention(qkv, bias_tiles, lqk, subln_g, *, heads, tq, lam_init):
    b, s, _ = qkv.shape
    nb = bias_tiles.shape[1]
    live_q_blocks = 4
    est = 2 * 4 * s * ATT_DV * 2 + 2 * nb * tq * tq * 4 + live_q_blocks * 2 * 2 * s * tq * 4
    kern = functools.partial(_attn_kernel, tq=tq, lam_init=lam_init)
    return pl.pallas_call(
        kern,
        grid=(b, heads),
        in_specs=[pl.BlockSpec((1, s, ATT_DV), lambda bi, h: (bi, 0, h)),
                  pl.BlockSpec((1, s, ATT_DV), lambda bi, h: (bi, 0, heads + h)),
                  pl.BlockSpec((1, s, ATT_DV), lambda bi, h: (bi, 0, 2 * heads + h)),
                  pl.BlockSpec((1, nb, tq, tq), lambda bi, h: (h, 0, 0, 0)),
                  pl.BlockSpec(lqk.shape, lambda bi, h: (0, 0)),
                  pl.BlockSpec((1, ATT_DV), lambda bi, h: (0, 0))],
        out_specs=pl.BlockSpec((1, s, ATT_DV), lambda bi, h: (bi, 0, h)),
        out_shape=jax.ShapeDtypeStruct((b, s, heads * ATT_DV), BF16),
        compiler_params=pltpu.CompilerParams(
            dimension_semantics=("parallel", "parallel"), vmem_limit_bytes=_vmem_limit(est)),
        name="diff_attention",
    )(qkv, qkv, qkv, bias_tiles, lqk, subln_g)


def _out_proj_kernel(x_ref, rec_ref, att_ref, w_ref, g_ref, b_ref, o_ref, ob_ref, *, alpha):
    w1 = rec_ref.shape[1]
    m = _dot(rec_ref[...], w_ref[0:w1, :]) + _dot(att_ref[...], w_ref[w1:, :])
    h = _layer_norm(alpha * x_ref[...] + m, g_ref[...], b_ref[...])
    o_ref[...] = h
    ob_ref[...] = h.astype(BF16)


def _out_proj_ln(x2d, rec2d, att2d, w_bf16, g, b, alpha):
    m, d = x2d.shape
    w1, w2 = rec2d.shape[1], att2d.shape[1]
    tm = _tile(m, 512)
    est = 2 * tm * d * 4 * 2 + 2 * tm * (w1 + w2) * 2 + 2 * (w1 + w2) * d * 2 + 2 * tm * d * 4 + 2 * tm * d * 2
    rowblk = lambda i: (i, 0)
    fixed = lambda i: (0, 0)
    return pl.pallas_call(
        functools.partial(_out_proj_kernel, alpha=alpha),
        grid=(m // tm,),
        in_specs=[pl.BlockSpec((tm, d), rowblk), pl.BlockSpec((tm, w1), rowblk), pl.BlockSpec((tm, w2), rowblk),
                  pl.BlockSpec((w1 + w2, d), fixed), pl.BlockSpec((1, d), fixed), pl.BlockSpec((1, d), fixed)],
        out_specs=[pl.BlockSpec((tm, d), rowblk), pl.BlockSpec((tm, d), rowblk)],
        out_shape=[jax.ShapeDtypeStruct((m, d), F32), jax.ShapeDtypeStruct((m, d), BF16)],
        compiler_params=pltpu.CompilerParams(
            dimension_semantics=("parallel",), vmem_limit_bytes=_vmem_limit(est)),
        name="out_proj_ln1",
    )(x2d, rec2d, att2d, w_bf16, g, b)


def _ffn_kernel(hb_ref, wg_ref, wu_ref, wd_ref, o_ref):
    @pl.when(pl.program_id(1) == 0)
    def _():
        o_ref[...] = jnp.zeros(o_ref.shape, F32)

    hb = hb_ref[...]
    gate = _dot(hb, wg_ref[...].astype(BF16))
    up = _dot(hb, wu_ref[...].astype(BF16))
    act = (gate * jax.nn.sigmoid(gate)) * up
    o_ref[...] += _dot(act.astype(BF16), wd_ref[...].astype(BF16))


def _ffn(hb2d, wg, wu, wd):
    m, d = hb2d.shape
    f = wg.shape[1]
    tm, tf = _tile(m, 1024), _tile(f, 512)
    est = 2 * tm * d * 2 + 2 * 3 * d * tf * 4 + 2 * tm * d * 4 + 3 * tm * tf * 4 + tm * d * 4
    rowblk = lambda i, j: (i, 0)
    return pl.pallas_call(
        _ffn_kernel,
        grid=(m // tm, f // tf),
        in_specs=[pl.BlockSpec((tm, d), rowblk),
                  pl.BlockSpec((d, tf), lambda i, j: (0, j)),
                  pl.BlockSpec((d, tf), lambda i, j: (0, j)),
                  pl.BlockSpec((tf, d), lambda i, j: (j, 0))],
        out_specs=pl.BlockSpec((tm, d), rowblk),
        out_shape=jax.ShapeDtypeStruct((m, d), F32),
        compiler_params=pltpu.CompilerParams(
            dimension_semantics=("parallel", "arbitrary"), vmem_limit_bytes=_vmem_limit(est)),
        name="swiglu",
    )(hb2d, wg, wu, wd)


def _ple_kernel(h_ref, f_ref, p_ref, g2_ref, b2_ref, wg_ref, bg_ref, wp_ref, g3_ref, b3_ref, o_ref, *, alpha):
    h = _layer_norm(alpha * h_ref[...] + f_ref[...], g2_ref[...], b2_ref[...])
    gate = _sigmoid_abs(_dot(h.astype(BF16), wg_ref[...]) + bg_ref[...])
    e = _dot(p_ref[...].astype(BF16), wp_ref[...])
    o_ref[...] = _layer_norm(alpha * h + gate * e, g3_ref[...], b3_ref[...])


def _ln2_ple_ln3(h2d, f2d, p2d, g2, b2, wg, bg, wp, g3, b3, alpha):
    m, d = h2d.shape
    pd = p2d.shape[1]
    tm = _tile(m, 512)
    est = 3 * 2 * tm * d * 4 + 2 * tm * pd * 4 + 2 * d * d * 2 + 2 * pd * d * 2 + 3 * tm * d * 4
    rowblk = lambda i: (i, 0)
    fixed = lambda i: (0, 0)
    vec = pl.BlockSpec((1, d), fixed)
    return pl.pallas_call(
        functools.partial(_ple_kernel, alpha=alpha),
        grid=(m // tm,),
        in_specs=[pl.BlockSpec((tm, d), rowblk), pl.BlockSpec((tm, d), rowblk), pl.BlockSpec((tm, pd), rowblk),
                  vec, vec, pl.BlockSpec((d, d), fixed), vec, pl.BlockSpec((pd, d), fixed), vec, vec],
        out_specs=pl.BlockSpec((tm, d), rowblk),
        out_shape=jax.ShapeDtypeStruct((m, d), F32),
        compiler_params=pltpu.CompilerParams(
            dimension_semantics=("parallel",), vmem_limit_bytes=_vmem_limit(est)),
        name="ln2_ple_ln3",
    )(h2d, f2d, p2d, g2, b2, wg, bg, wp, g3, b3)


def kernel(x, p, w_in, conv_w, conv_b, lru_wa, lru_ba, lru_wx, lru_bx, lru_lambda, diff_lq1, diff_lk1, diff_lq2, diff_lk2, diff_subln_g, rel_bias, w_out, ln1_g, ln1_b, w_ffn_gate, w_ffn_up, w_ffn_down, ln2_g, ln2_b, w_ple_gate, b_ple_gate, w_ple_proj, ln3_g, ln3_b):
    batch, seq, d = x.shape
    depth = w_in.shape[0]
    nblk, bw = lru_wa.shape[1], lru_wa.shape[2]
    lru_w = nblk * bw
    heads = rel_bias.shape[1]
    att_qk = 2 * heads * ATT_DQK
    assert w_in.shape[2] == 2 * lru_w + 2 * att_qk + heads * ATT_DV
    assert lru_w % ATT_DV == 0 and att_qk == heads * ATT_DV
    m = batch * seq
    alpha = (2 * depth) ** 0.25

    tq = _tile(seq, 256)
    bias_tiles = _bias_tiles(rel_bias, tq)
    row = lambda v: v.reshape(1, -1)
    col = jnp.arange(2 * att_qk + heads * ATT_DV)
    qkv_scale = row(jnp.where(col < att_qk, ATT_DQK ** -0.5 * LOG2_E, 1.0).astype(F32))

    h = x.reshape(m, d)
    for i in range(depth):
        lam_init = 0.8 - 0.6 * math.exp(-0.3 * i)
        wgate = jnp.concatenate([lru_wa[i], lru_wx[i]], axis=-1).astype(BF16)
        qkv, rec = _in_proj_lru(h.reshape(batch, seq, d), w_in[i].astype(BF16), qkv_scale, conv_w[i], row(conv_b[i]),
                                wgate, row(lru_ba[i]), row(lru_bx[i]), row(lru_lambda[i]))
        lqk = jnp.stack([diff_lq1[i], diff_lk1[i], diff_lq2[i], diff_lk2[i]])
        att = _diff_attention(qkv, bias_tiles, lqk, row(diff_subln_g[i]), heads=heads, tq=tq, lam_init=lam_init)
        h, hb = _out_proj_ln(h, rec.reshape(m, lru_w), att.reshape(m, heads * ATT_DV), w_out[i].astype(BF16),
                             row(ln1_g[i]), row(ln1_b[i]), alpha)
        f = _ffn(hb, w_ffn_gate[i], w_ffn_up[i], w_ffn_down[i])
        h = _ln2_ple_ln3(h, f, p[i].reshape(m, -1), row(ln2_g[i]), row(ln2_b[i]), w_ple_gate[i].astype(BF16),
                         row(b_ple_gate[i]), w_ple_proj[i].astype(BF16), row(ln3_g[i]), row(ln3_b[i]), alpha)
    return h.reshape(batch, seq, d)
```

```python
import functools
import math

import jax
import jax.numpy as jnp
import numpy as np
from jax import lax
from jax.experimental import pallas as pl
from jax.experimental.pallas import tpu as pltpu

F32 = jnp.float32
BF16 = jnp.bfloat16

LN_EPS = 1e-5
LRU_C = 8.0
CONV_W = 4
ATT_DV = 128
ATT_DQK = ATT_DV // 2
REL_BUCKETS = 32
REL_MAX_DIST = 128
ATT_PIPELINE_DEPTH = 2
LOG2_E = math.log2(math.e)

V7X_LANES = 128
V7X_SUBLANES = 8
BF16_SUBLANES = 2 * V7X_SUBLANES
V7X_VMEM_BYTES = 64 * 1024 * 1024
V7X_VMEM_RESERVE_BYTES = 6 * 1024 * 1024


def _vmem_limit(estimate_bytes):
    return int(min(V7X_VMEM_BYTES - V7X_VMEM_RESERVE_BYTES, max(2 * estimate_bytes, 16 * 1024 * 1024)))


def _tile(dim, preferred):
    if dim % preferred == 0:
        return preferred
    return dim


def _layer_norm(y, g, b):
    mu = jnp.mean(y, axis=-1, keepdims=True)
    yc = y - mu
    var = jnp.mean(yc * yc, axis=-1, keepdims=True)
    return yc * lax.rsqrt(var + LN_EPS) * g + b


def _dot(a, b):
    return jnp.dot(a, b, preferred_element_type=F32)


def _gelu_tanh(x):
    return 0.5 * x * (1.0 + jnp.tanh(math.sqrt(2.0 / math.pi) * (x + 0.044715 * (x * x * x))))


def _sigmoid_abs(x):
    return 0.5 * jnp.tanh(0.5 * x) + 0.5


def _in_proj_lru_kernel(x_ref, w_ref, cs_ref, cw_ref, cb_ref, wg_ref, ba_ref, bx_ref, lam_ref,
                        qkv_ref, rec_ref, xext_ref, a_ref, u_ref, h_ref, *, chunk):
    nbatch, tt, d = x_ref.shape
    rows = nbatch * tt
    lru_w = rec_ref.shape[2]
    n_qkv = qkv_ref.shape[2]
    bw = wg_ref.shape[1]
    halo = V7X_SUBLANES
    pitch = xext_ref.shape[1]
    half = V7X_SUBLANES

    @pl.when(pl.program_id(0) == 0)
    def _():
        xext_ref[:, 0:halo, :] = jnp.zeros((nbatch, halo, lru_w), F32)
        h_ref[...] = jnp.zeros(h_ref.shape, F32)

    xb = x_ref[...].reshape(rows, d).astype(BF16)
    xext_ref[:, halo:halo + tt, :] = _dot(xb, w_ref[:, 0:lru_w]).reshape(nbatch, tt, lru_w)
    yg = _dot(xb, w_ref[:, lru_w:2 * lru_w])
    for c0 in range(0, n_qkv, chunk):
        val = _dot(xb, w_ref[:, 2 * lru_w + c0:2 * lru_w + c0 + chunk]) * cs_ref[:, c0:c0 + chunk]
        qkv_ref[:, :, c0:c0 + chunk] = val.astype(BF16).reshape(nbatch, tt, chunk)

    for n in range(lru_w // bw):
        cols = slice(n * bw, (n + 1) * bw)
        xc = cb_ref[:, cols]
        for t in range(CONV_W):
            start = halo - (CONV_W - 1) + t
            xc = xc + xext_ref[:, start:start + tt, cols].reshape(rows, bw) * cw_ref[t:t + 1, cols]
        gates = _dot(xc.astype(BF16), wg_ref[n])
        gate_a = jax.nn.sigmoid(gates[:, :bw] + ba_ref[:, cols])
        gate_x = _sigmoid_abs(gates[:, bw:] + bx_ref[:, cols])
        neg_lam = -lam_ref[:, cols]
        softplus = jnp.maximum(neg_lam, 0.0) + jnp.log1p(jnp.exp(-jnp.abs(neg_lam)))
        log_a = -LRU_C * gate_a * softplus
        a = jnp.exp(log_a)
        one_minus_a2 = -jnp.tanh(log_a) * (a * a + 1.0)
        mult = jnp.where(one_minus_a2 > 0.0, one_minus_a2 * lax.rsqrt(one_minus_a2), 0.0)
        u = mult * (gate_x * xc)
        for bi in range(nbatch):
            a_ref[n, bi * pitch:bi * pitch + tt, :] = a[bi * tt:(bi + 1) * tt]
            u_ref[n, bi * pitch:bi * pitch + tt, :] = u[bi * tt:(bi + 1) * tt]

        for g in range(nbatch // half):
            h = h_ref[g * half:(g + 1) * half, cols]
            for t in range(tt):
                step = pl.ds(g * half * pitch + t, half, stride=pitch)
                h = a_ref[n, step, :] * h + u_ref[n, step, :]
                u_ref[n, step, :] = h
            h_ref[g * half:(g + 1) * half, cols] = h

        hs = jnp.concatenate([u_ref[n, bi * pitch:bi * pitch + tt, :] for bi in range(nbatch)], axis=0)
        rec_ref[:, :, cols] = (hs * _gelu_tanh(yg[:, cols])).astype(BF16).reshape(nbatch, tt, bw)

    xext_ref[:, 0:halo, :] = xext_ref[:, tt:tt + halo, :]


def _in_proj_lru(x, w_bf16, qkv_scale, conv_w, conv_b, wgate, ba, bx, lam):
    nbatch, seq, d = x.shape
    nblk, bw, _ = wgate.shape
    lru_w = nblk * bw
    n_in = w_bf16.shape[1]
    n_qkv = n_in - 2 * lru_w
    tt = _tile(seq, 32)
    rows = nbatch * tt
    pitch = tt + V7X_SUBLANES
    chunk = _tile(n_qkv, 1024)
    assert nbatch % V7X_SUBLANES == 0 and tt % BF16_SUBLANES == 0
    est = (d * n_in * 2 + 2 * rows * d * 4 + 2 * rows * (n_qkv + lru_w) * 2 + 3 * nbatch * pitch * lru_w * 4
           + rows * d * 2 + 3 * rows * lru_w * 4 + 2 * rows * chunk * 4)
    fixed = lambda t: (0, 0)
    kern = functools.partial(_in_proj_lru_kernel, chunk=chunk)
    return pl.pallas_call(
        kern,
        grid=(seq // tt,),
        in_specs=[pl.BlockSpec((nbatch, tt, d), lambda t: (0, t, 0)),
                  pl.BlockSpec((d, n_in), fixed, pipeline_mode=pl.Buffered(1)),
                  pl.BlockSpec((1, n_qkv), fixed),
                  pl.BlockSpec((CONV_W, lru_w), fixed),
                  pl.BlockSpec((1, lru_w), fixed),
                  pl.BlockSpec((nblk, bw, 2 * bw), lambda t: (0, 0, 0)),
                  pl.BlockSpec((1, lru_w), fixed),
                  pl.BlockSpec((1, lru_w), fixed),
                  pl.BlockSpec((1, lru_w), fixed)],
        out_specs=[pl.BlockSpec((nbatch, tt, n_qkv), lambda t: (0, t, 0)),
                   pl.BlockSpec((nbatch, tt, lru_w), lambda t: (0, t, 0))],
        out_shape=[jax.ShapeDtypeStruct((nbatch, seq, n_qkv), BF16),
                   jax.ShapeDtypeStruct((nbatch, seq, lru_w), BF16)],
        scratch_shapes=[pltpu.VMEM((nbatch, pitch, lru_w), F32),
                        pltpu.VMEM((nblk, nbatch * pitch, bw), F32),
                        pltpu.VMEM((nblk, nbatch * pitch, bw), F32),
                        pltpu.VMEM((nbatch, lru_w), F32)],
        compiler_params=pltpu.CompilerParams(
            dimension_semantics=("arbitrary",), vmem_limit_bytes=_vmem_limit(est)),
        name="in_proj_rg_lru",
    )(x, w_bf16, qkv_scale, conv_w, conv_b, wgate, ba, bx, lam)


def _bucket_thresholds():
    n = np.arange(0, 8 * REL_MAX_DIST)
    max_exact = REL_BUCKETS // 2
    nf = np.maximum(n, 1).astype(np.float32)
    large = max_exact + (np.log(nf / np.float32(max_exact)) / np.float32(math.log(REL_MAX_DIST / max_exact))
                         * np.float32(REL_BUCKETS - max_exact)).astype(np.int32)
    bucket = np.where(n < max_exact, n, np.minimum(large, REL_BUCKETS - 1))
    assert np.all(np.diff(bucket) >= 0) and bucket[-1] == REL_BUCKETS - 1
    return [int(np.argmax(bucket >= j)) for j in range(1, REL_BUCKETS)]


def _near_key_blocks(tq):
    last = _bucket_thresholds()[-1]
    d = 0
    while d * tq - (tq - 1) < last:
        d += 1
    return d


def _bias_kernel(table_ref, o_ref, *, thresholds):
    h = pl.program_id(0)
    nb, tk, tq = o_ref.shape[1], o_ref.shape[2], o_ref.shape[3]
    key = lax.broadcasted_iota(jnp.int32, (tk, tq), 0)
    qry = lax.broadcasted_iota(jnp.int32, (tk, tq), 1)
    far = table_ref[REL_BUCKETS - 1, h]
    for d in range(nb):
        dist = d * tk + qry - key
        val = jnp.full((tk, tq), table_ref[0, h], F32)
        for j, thr in enumerate(thresholds):
            val = jnp.where(dist >= thr, table_ref[j + 1, h], val)
        o_ref[0, d] = jnp.where(dist < 0, -jnp.inf, (val - far) * LOG2_E)


def _bias_tiles(rel_bias, tq):
    heads = rel_bias.shape[1]
    nb = _near_key_blocks(tq)
    return pl.pallas_call(
        functools.partial(_bias_kernel, thresholds=_bucket_thresholds()),
        grid=(heads,),
        in_specs=[pl.BlockSpec(memory_space=pltpu.SMEM)],
        out_specs=pl.BlockSpec((1, nb, tq, tq), lambda h: (h, 0, 0, 0)),
        out_shape=jax.ShapeDtypeStruct((heads, nb, tq, tq), F32),
        compiler_params=pltpu.CompilerParams(
            dimension_semantics=("parallel",), vmem_limit_bytes=_vmem_limit(2 * nb * tq * tq * 4)),
        name="rel_bias_tiles",
    )(rel_bias)


def _attn_kernel(q_ref, k_ref, v_ref, bias_ref, lqk_ref, g_ref, o_ref, *, tq, lam_init):
    seq = q_ref.shape[1]
    nb = bias_ref.shape[1]
    nt_dims = (((1,), (1,)), ((), ()))

    lqk = lqk_ref[...]
    lam = (jnp.exp(jnp.sum(lqk[0:1] * lqk[1:2], axis=-1, keepdims=True))
           - jnp.exp(jnp.sum(lqk[2:3] * lqk[3:4], axis=-1, keepdims=True)) + lam_init)
    ones_rows = (lax.broadcasted_iota(jnp.int32, (BF16_SUBLANES, seq), 0) == 0).astype(F32)
    v_aug_t = jnp.concatenate([v_ref[0].astype(F32).T, ones_rows], axis=0).astype(BF16)
    lane = lax.broadcasted_iota(jnp.int32, (tq, ATT_DV), 1)

    def scores(i):
        q = q_ref[0, i * tq:(i + 1) * tq, :]
        k = k_ref[0, 0:(i + 1) * tq, :]
        n_far = max(i + 1 - nb, 0)
        out = []
        for c in range(2):
            q_c = jnp.where((lane >= ATT_DQK) if c else (lane < ATT_DQK), q, jnp.zeros_like(q))
            s_t = lax.dot_general(k, q_c, nt_dims, preferred_element_type=F32)
            parts = [s_t[0:n_far * tq]] if n_far else []
            for j in range(n_far, i + 1):
                parts.append(s_t[j * tq:(j + 1) * tq] + bias_ref[0, i - j])
            m = functools.reduce(jnp.maximum, [jnp.max(part, axis=0, keepdims=True) for part in parts])
            out.append((parts, m))
        return out

    n_q = seq // tq
    queue = [scores(i) for i in range(min(ATT_PIPELINE_DEPTH, n_q))]
    for i in range(n_q):
        current = queue.pop(0)
        if i + ATT_PIPELINE_DEPTH < n_q:
            queue.append(scores(i + ATT_PIPELINE_DEPTH))
        n_keys = (i + 1) * tq
        probs = [jnp.concatenate([jnp.exp2(part - m).astype(BF16) for part in parts], axis=0) for parts, m in current]
        pv = _dot(v_aug_t[:, 0:n_keys], jnp.concatenate(probs, axis=1))
        num1, num2 = pv[0:ATT_DV, 0:tq], pv[0:ATT_DV, tq:]
        sum1, sum2 = pv[ATT_DV:ATT_DV + 1, 0:tq], pv[ATT_DV:ATT_DV + 1, tq:]
        out_t = num1 * (1.0 / sum1) - num2 * (lam / sum2)
        rms = lax.rsqrt(jnp.mean(out_t * out_t, axis=0, keepdims=True) + LN_EPS)
        o_ref[0, i * tq:(i + 1) * tq, :] = ((out_t * rms).T * g_ref[...] * (1.0 - lam_init)).astype(o_ref.dtype)


def _diff_attention(qkv, bias_tiles, lqk, subln_g, *, heads, tq, lam_init):
    b, s, _ = qkv.shape
    nb = bias_tiles.shape[1]
    live_q_blocks = 4
    est = 2 * 4 * s * ATT_DV * 2 + 2 * nb * tq * tq * 4 + live_q_blocks * 2 * 2 * s * tq * 4
    kern = functools.partial(_attn_kernel, tq=tq, lam_init=lam_init)
    return pl.pallas_call(
        kern,
        grid=(b, heads),
        in_specs=[pl.BlockSpec((1, s, ATT_DV), lambda bi, h: (bi, 0, h)),
                  pl.BlockSpec((1, s, ATT_DV), lambda bi, h: (bi, 0, heads + h)),
                  pl.BlockSpec((1, s, ATT_DV), lambda bi, h: (bi, 0, 2 * heads + h)),
                  pl.BlockSpec((1, nb, tq, tq), lambda bi, h: (h, 0, 0, 0)),
                  pl.BlockSpec(lqk.shape, lambda bi, h: (0, 0)),
                  pl.BlockSpec((1, ATT_DV), lambda bi, h: (0, 0))],
        out_specs=pl.BlockSpec((1, s, ATT_DV), lambda bi, h: (bi, 0, h)),
        out_shape=jax.ShapeDtypeStruct((b, s, heads * ATT_DV), BF16),
        compiler_params=pltpu.CompilerParams(
            dimension_semantics=("parallel", "parallel"), vmem_limit_bytes=_vmem_limit(est)),
        name="diff_attention",
    )(qkv, qkv, qkv, bias_tiles, lqk, subln_g)


def _out_proj_kernel(x_ref, rec_ref, att_ref, w_ref, g_ref, b_ref, o_ref, ob_ref, *, alpha):
    w1 = rec_ref.shape[1]
    m = _dot(rec_ref[...], w_ref[0:w1, :]) + _dot(att_ref[...], w_ref[w1:, :])
    h = _layer_norm(alpha * x_ref[...] + m, g_ref[...], b_ref[...])
    o_ref[...] = h
    ob_ref[...] = h.astype(BF16)


def _out_proj_ln(x2d, rec2d, att2d, w_bf16, g, b, alpha):
    m, d = x2d.shape
    w1, w2 = rec2d.shape[1], att2d.shape[1]
    tm = _tile(m, 512)
    est = 2 * tm * d * 4 * 2 + 2 * tm * (w1 + w2) * 2 + 2 * (w1 + w2) * d * 2 + 2 * tm * d * 4 + 2 * tm * d * 2
    rowblk = lambda i: (i, 0)
    fixed = lambda i: (0, 0)
    return pl.pallas_call(
        functools.partial(_out_proj_kernel, alpha=alpha),
        grid=(m // tm,),
        in_specs=[pl.BlockSpec((tm, d), rowblk), pl.BlockSpec((tm, w1), rowblk), pl.BlockSpec((tm, w2), rowblk),
                  pl.BlockSpec((w1 + w2, d), fixed), pl.BlockSpec((1, d), fixed), pl.BlockSpec((1, d), fixed)],
        out_specs=[pl.BlockSpec((tm, d), rowblk), pl.BlockSpec((tm, d), rowblk)],
        out_shape=[jax.ShapeDtypeStruct((m, d), F32), jax.ShapeDtypeStruct((m, d), BF16)],
        compiler_params=pltpu.CompilerParams(
            dimension_semantics=("parallel",), vmem_limit_bytes=_vmem_limit(est)),
        name="out_proj_ln1",
    )(x2d, rec2d, att2d, w_bf16, g, b)


def _ffn_kernel(hb_ref, wg_ref, wu_ref, wd_ref, o_ref):
    @pl.when(pl.program_id(1) == 0)
    def _():
        o_ref[...] = jnp.zeros(o_ref.shape, F32)

    hb = hb_ref[...]
    gate = _dot(hb, wg_ref[...].astype(BF16))
    up = _dot(hb, wu_ref[...].astype(BF16))
    act = (gate * _sigmoid_abs(gate)) * up
    o_ref[...] += _dot(act.astype(BF16), wd_ref[...].astype(BF16))


def _ffn(hb2d, wg, wu, wd):
    m, d = hb2d.shape
    f = wg.shape[1]
    tm, tf = _tile(m, 1024), _tile(f, 512)
    est = 2 * tm * d * 2 + 2 * 3 * d * tf * 4 + 2 * tm * d * 4 + 3 * tm * tf * 4 + tm * d * 4
    rowblk = lambda i, j: (i, 0)
    return pl.pallas_call(
        _ffn_kernel,
        grid=(m // tm, f // tf),
        in_specs=[pl.BlockSpec((tm, d), rowblk),
                  pl.BlockSpec((d, tf), lambda i, j: (0, j)),
                  pl.BlockSpec((d, tf), lambda i, j: (0, j)),
                  pl.BlockSpec((tf, d), lambda i, j: (j, 0))],
        out_specs=pl.BlockSpec((tm, d), rowblk),
        out_shape=jax.ShapeDtypeStruct((m, d), F32),
        compiler_params=pltpu.CompilerParams(
            dimension_semantics=("parallel", "arbitrary"), vmem_limit_bytes=_vmem_limit(est)),
        name="swiglu",
    )(hb2d, wg, wu, wd)


def _ple_kernel(h_ref, f_ref, p_ref, g2_ref, b2_ref, wg_ref, bg_ref, wp_ref, g3_ref, b3_ref, o_ref, *, alpha):
    h = _layer_norm(alpha * h_ref[...] + f_ref[...], g2_ref[...], b2_ref[...])
    gate = _sigmoid_abs(_dot(h.astype(BF16), wg_ref[...]) + bg_ref[...])
    e = _dot(p_ref[...].astype(BF16), wp_ref[...])
    o_ref[...] = _layer_norm(alpha * h + gate * e, g3_ref[...], b3_ref[...])


def _ln2_ple_ln3(h2d, f2d, p2d, g2, b2, wg, bg, wp, g3, b3, alpha):
    m, d = h2d.shape
    pd = p2d.shape[1]
    tm = _tile(m, 512)
    est = 3 * 2 * tm * d * 4 + 2 * tm * pd * 4 + 2 * d * d * 2 + 2 * pd * d * 2 + 3 * tm * d * 4
    rowblk = lambda i: (i, 0)
    fixed = lambda i: (0, 0)
    vec = pl.BlockSpec((1, d), fixed)
    return pl.pallas_call(
        functools.partial(_ple_kernel, alpha=alpha),
        grid=(m // tm,),
        in_specs=[pl.BlockSpec((tm, d), rowblk), pl.BlockSpec((tm, d), rowblk), pl.BlockSpec((tm, pd), rowblk),
                  vec, vec, pl.BlockSpec((d, d), fixed), vec, pl.BlockSpec((pd, d), fixed), vec, vec],
        out_specs=pl.BlockSpec((tm, d), rowblk),
        out_shape=jax.ShapeDtypeStruct((m, d), F32),
        compiler_params=pltpu.CompilerParams(
            dimension_semantics=("parallel",), vmem_limit_bytes=_vmem_limit(est)),
        name="ln2_ple_ln3",
    )(h2d, f2d, p2d, g2, b2, wg, bg, wp, g3, b3)


def kernel(x, p, w_in, conv_w, conv_b, lru_wa, lru_ba, lru_wx, lru_bx, lru_lambda, diff_lq1, diff_lk1, diff_lq2, diff_lk2, diff_subln_g, rel_bias, w_out, ln1_g, ln1_b, w_ffn_gate, w_ffn_up, w_ffn_down, ln2_g, ln2_b, w_ple_gate, b_ple_gate, w_ple_proj, ln3_g, ln3_b):
    batch, seq, d = x.shape
    depth = w_in.shape[0]
    nblk, bw = lru_wa.shape[1], lru_wa.shape[2]
    lru_w = nblk * bw
    heads = rel_bias.shape[1]
    att_qk = 2 * heads * ATT_DQK
    assert w_in.shape[2] == 2 * lru_w + 2 * att_qk + heads * ATT_DV
    assert lru_w % ATT_DV == 0 and att_qk == heads * ATT_DV
    m = batch * seq
    alpha = (2 * depth) ** 0.25

    tq = _tile(seq, 256)
    bias_tiles = _bias_tiles(rel_bias, tq)
    row = lambda v: v.reshape(1, -1)
    col = jnp.arange(2 * att_qk + heads * ATT_DV)
    qkv_scale = row(jnp.where(col < att_qk, ATT_DQK ** -0.5 * LOG2_E, 1.0).astype(F32))

    h = x.reshape(m, d)
    for i in range(depth):
        lam_init = 0.8 - 0.6 * math.exp(-0.3 * i)
        wgate = jnp.concatenate([lru_wa[i], lru_wx[i]], axis=-1).astype(BF16)
        qkv, rec = _in_proj_lru(h.reshape(batch, seq, d), w_in[i].astype(BF16), qkv_scale, conv_w[i], row(conv_b[i]),
                                wgate, row(lru_ba[i]), row(lru_bx[i]), row(lru_lambda[i]))
        lqk = jnp.stack([diff_lq1[i], diff_lk1[i], diff_lq2[i], diff_lk2[i]])
        att = _diff_attention(qkv, bias_tiles, lqk, row(diff_subln_g[i]), heads=heads, tq=tq, lam_init=lam_init)
        h, hb = _out_proj_ln(h, rec.reshape(m, lru_w), att.reshape(m, heads * ATT_DV), w_out[i].astype(BF16),
                             row(ln1_g[i]), row(ln1_b[i]), alpha)
        f = _ffn(hb, w_ffn_gate[i], w_ffn_up[i], w_ffn_down[i])
        h = _ln2_ple_ln3(h, f, p[i].reshape(m, -1), row(ln2_g[i]), row(ln2_b[i]), w_ple_gate[i].astype(BF16),
                         row(b_ple_gate[i]), w_ple_proj[i].astype(BF16), row(ln3_g[i]), row(ln3_b[i]), alpha)
    return h.reshape(batch, seq, d)
```

```python
import functools
import math

import jax
import jax.numpy as jnp
import numpy as np
from jax import lax
from jax.experimental import pallas as pl
from jax.experimental.pallas import tpu as pltpu

F32 = jnp.float32
BF16 = jnp.bfloat16

LN_EPS = 1e-5
LRU_C = 8.0
CONV_W = 4
ATT_DV = 128
ATT_DQK = ATT_DV // 2
REL_BUCKETS = 32
REL_MAX_DIST = 128
ATT_PIPELINE_DEPTH = 2
LOG2_E = math.log2(math.e)

V7X_LANES = 128
V7X_SUBLANES = 8
BF16_SUBLANES = 2 * V7X_SUBLANES
V7X_VMEM_BYTES = 64 * 1024 * 1024
V7X_VMEM_RESERVE_BYTES = 6 * 1024 * 1024


def _vmem_limit(estimate_bytes):
    return int(min(V7X_VMEM_BYTES - V7X_VMEM_RESERVE_BYTES, max(2 * estimate_bytes, 16 * 1024 * 1024)))


def _tile(dim, preferred):
    if dim % preferred == 0:
        return preferred
    return dim


def _layer_norm(y, g, b):
    mu = jnp.mean(y, axis=-1, keepdims=True)
    yc = y - mu
    var = jnp.mean(yc * yc, axis=-1, keepdims=True)
    return yc * lax.rsqrt(var + LN_EPS) * g + b


def _dot(a, b):
    return jnp.dot(a, b, preferred_element_type=F32)


def _gelu_tanh(x):
    return 0.5 * x * (1.0 + jnp.tanh(math.sqrt(2.0 / math.pi) * (x + 0.044715 * (x * x * x))))


def _sigmoid_abs(x):
    return 0.5 * jnp.tanh(0.5 * x) + 0.5


def _in_proj_lru_kernel(x_ref, w_ref, cs_ref, cw_ref, cb_ref, wg_ref, ba_ref, bx_ref, lam_ref,
                        qkv_ref, rec_ref, xext_ref, a_ref, u_ref, h_ref, *, chunk):
    nbatch, tt, d = x_ref.shape
    rows = nbatch * tt
    lru_w = rec_ref.shape[2]
    n_qkv = qkv_ref.shape[2]
    bw = wg_ref.shape[1]
    halo = V7X_SUBLANES
    pitch = xext_ref.shape[1]
    half = V7X_SUBLANES

    @pl.when(pl.program_id(0) == 0)
    def _():
        xext_ref[:, 0:halo, :] = jnp.zeros((nbatch, halo, lru_w), F32)
        h_ref[...] = jnp.zeros(h_ref.shape, F32)

    xb = x_ref[...].reshape(rows, d).astype(BF16)
    xext_ref[:, halo:halo + tt, :] = _dot(xb, w_ref[:, 0:lru_w]).reshape(nbatch, tt, lru_w)
    yg = _dot(xb, w_ref[:, lru_w:2 * lru_w])
    for c0 in range(0, n_qkv, chunk):
        val = _dot(xb, w_ref[:, 2 * lru_w + c0:2 * lru_w + c0 + chunk]) * cs_ref[:, c0:c0 + chunk]
        qkv_ref[:, :, c0:c0 + chunk] = val.astype(BF16).reshape(nbatch, tt, chunk)

    for n in range(lru_w // bw):
        cols = slice(n * bw, (n + 1) * bw)
        xc = cb_ref[:, cols]
        for t in range(CONV_W):
            start = halo - (CONV_W - 1) + t
            xc = xc + xext_ref[:, start:start + tt, cols].reshape(rows, bw) * cw_ref[t:t + 1, cols]
        gates = _dot(xc.astype(BF16), wg_ref[n])
        gate_a = jax.nn.sigmoid(gates[:, :bw] + ba_ref[:, cols])
        gate_x = _sigmoid_abs(gates[:, bw:] + bx_ref[:, cols])
        neg_lam = -lam_ref[:, cols]
        softplus = jnp.maximum(neg_lam, 0.0) + jnp.log1p(jnp.exp(-jnp.abs(neg_lam)))
        log_a = -LRU_C * gate_a * softplus
        a = jnp.exp(log_a)
        one_minus_a2 = -jnp.tanh(log_a) * (a * a + 1.0)
        mult = jnp.where(one_minus_a2 > 0.0, one_minus_a2 * lax.rsqrt(one_minus_a2), 0.0)
        u = mult * (gate_x * xc)
        for bi in range(nbatch):
            a_ref[n, bi * pitch:bi * pitch + tt, :] = a[bi * tt:(bi + 1) * tt]
            u_ref[n, bi * pitch:bi * pitch + tt, :] = u[bi * tt:(bi + 1) * tt]

        for g in range(nbatch // half):
            h = h_ref[g * half:(g + 1) * half, cols]
            for t in range(tt):
                step = pl.ds(g * half * pitch + t, half, stride=pitch)
                h = a_ref[n, step, :] * h + u_ref[n, step, :]
                u_ref[n, step, :] = h
            h_ref[g * half:(g + 1) * half, cols] = h

        hs = jnp.concatenate([u_ref[n, bi * pitch:bi * pitch + tt, :] for bi in range(nbatch)], axis=0)
        rec_ref[:, :, cols] = (hs * _gelu_tanh(yg[:, cols])).astype(BF16).reshape(nbatch, tt, bw)

    xext_ref[:, 0:halo, :] = xext_ref[:, tt:tt + halo, :]


def _in_proj_lru(x, w_bf16, qkv_scale, conv_w, conv_b, wgate, ba, bx, lam):
    nbatch, seq, d = x.shape
    nblk, bw, _ = wgate.shape
    lru_w = nblk * bw
    n_in = w_bf16.shape[1]
    n_qkv = n_in - 2 * lru_w
    tt = _tile(seq, 32)
    rows = nbatch * tt
    pitch = tt + V7X_SUBLANES
    chunk = _tile(n_qkv, 1024)
    assert nbatch % V7X_SUBLANES == 0 and tt % BF16_SUBLANES == 0
    est = (d * n_in * 2 + 2 * rows * d * 4 + 2 * rows * (n_qkv + lru_w) * 2 + 3 * nbatch * pitch * lru_w * 4
           + rows * d * 2 + 3 * rows * lru_w * 4 + 2 * rows * chunk * 4)
    fixed = lambda t: (0, 0)
    kern = functools.partial(_in_proj_lru_kernel, chunk=chunk)
    return pl.pallas_call(
        kern,
        grid=(seq // tt,),
        in_specs=[pl.BlockSpec((nbatch, tt, d), lambda t: (0, t, 0)),
                  pl.BlockSpec((d, n_in), fixed, pipeline_mode=pl.Buffered(1)),
                  pl.BlockSpec((1, n_qkv), fixed),
                  pl.BlockSpec((CONV_W, lru_w), fixed),
                  pl.BlockSpec((1, lru_w), fixed),
                  pl.BlockSpec((nblk, bw, 2 * bw), lambda t: (0, 0, 0)),
                  pl.BlockSpec((1, lru_w), fixed),
                  pl.BlockSpec((1, lru_w), fixed),
                  pl.BlockSpec((1, lru_w), fixed)],
        out_specs=[pl.BlockSpec((nbatch, tt, n_qkv), lambda t: (0, t, 0)),
                   pl.BlockSpec((nbatch, tt, lru_w), lambda t: (0, t, 0))],
        out_shape=[jax.ShapeDtypeStruct((nbatch, seq, n_qkv), BF16),
                   jax.ShapeDtypeStruct((nbatch, seq, lru_w), BF16)],
        scratch_shapes=[pltpu.VMEM((nbatch, pitch, lru_w), F32),
                        pltpu.VMEM((nblk, nbatch * pitch, bw), F32),
                        pltpu.VMEM((nblk, nbatch * pitch, bw), F32),
                        pltpu.VMEM((nbatch, lru_w), F32)],
        compiler_params=pltpu.CompilerParams(
            dimension_semantics=("arbitrary",), vmem_limit_bytes=_vmem_limit(est)),
        name="in_proj_rg_lru",
    )(x, w_bf16, qkv_scale, conv_w, conv_b, wgate, ba, bx, lam)


def _bucket_thresholds():
    n = np.arange(0, 8 * REL_MAX_DIST)
    max_exact = REL_BUCKETS // 2
    nf = np.maximum(n, 1).astype(np.float32)
    large = max_exact + (np.log(nf / np.float32(max_exact)) / np.float32(math.log(REL_MAX_DIST / max_exact))
                         * np.float32(REL_BUCKETS - max_exact)).astype(np.int32)
    bucket = np.where(n < max_exact, n, np.minimum(large, REL_BUCKETS - 1))
    assert np.all(np.diff(bucket) >= 0) and bucket[-1] == REL_BUCKETS - 1
    return [int(np.argmax(bucket >= j)) for j in range(1, REL_BUCKETS)]


def _near_key_blocks(tq):
    last = _bucket_thresholds()[-1]
    d = 0
    while d * tq - (tq - 1) < last:
        d += 1
    return d


def _bias_kernel(table_ref, o_ref, *, thresholds):
    h = pl.program_id(0)
    nb, tk, tq = o_ref.shape[1], o_ref.shape[2], o_ref.shape[3]
    key = lax.broadcasted_iota(jnp.int32, (tk, tq), 0)
    qry = lax.broadcasted_iota(jnp.int32, (tk, tq), 1)
    far = table_ref[REL_BUCKETS - 1, h]
    for d in range(nb):
        dist = d * tk + qry - key
        val = jnp.full((tk, tq), table_ref[0, h], F32)
        for j, thr in enumerate(thresholds):
            val = jnp.where(dist >= thr, table_ref[j + 1, h], val)
        o_ref[0, d] = jnp.where(dist < 0, -jnp.inf, (val - far) * LOG2_E)


def _bias_tiles(rel_bias, tq):
    heads = rel_bias.shape[1]
    nb = _near_key_blocks(tq)
    return pl.pallas_call(
        functools.partial(_bias_kernel, thresholds=_bucket_thresholds()),
        grid=(heads,),
        in_specs=[pl.BlockSpec(memory_space=pltpu.SMEM)],
        out_specs=pl.BlockSpec((1, nb, tq, tq), lambda h: (h, 0, 0, 0)),
        out_shape=jax.ShapeDtypeStruct((heads, nb, tq, tq), F32),
        compiler_params=pltpu.CompilerParams(
            dimension_semantics=("parallel",), vmem_limit_bytes=_vmem_limit(2 * nb * tq * tq * 4)),
        name="rel_bias_tiles",
    )(rel_bias)


def _attn_kernel(q_ref, k_ref, v_ref, bias_ref, lqk_ref, g_ref, o_ref, *, tq, lam_init):
    seq = q_ref.shape[1]
    nb = bias_ref.shape[1]
    nt_dims = (((1,), (1,)), ((), ()))

    lqk = lqk_ref[...]
    lam = (jnp.exp(jnp.sum(lqk[0:1] * lqk[1:2], axis=-1, keepdims=True))
           - jnp.exp(jnp.sum(lqk[2:3] * lqk[3:4], axis=-1, keepdims=True)) + lam_init)
    ones_rows = (lax.broadcasted_iota(jnp.int32, (BF16_SUBLANES, seq), 0) == 0).astype(F32)
    v_aug_t = jnp.concatenate([v_ref[0].astype(F32).T, ones_rows], axis=0).astype(BF16)
    lane = lax.broadcasted_iota(jnp.int32, (tq, ATT_DV), 1)

    def scores(i):
        q = q_ref[0, i * tq:(i + 1) * tq, :]
        k = k_ref[0, 0:(i + 1) * tq, :]
        n_far = max(i + 1 - nb, 0)
        out = []
        for c in range(2):
            q_c = jnp.where((lane >= ATT_DQK) if c else (lane < ATT_DQK), q, jnp.zeros_like(q))
            s_t = lax.dot_general(k, q_c, nt_dims, preferred_element_type=F32)
            parts = [s_t[0:n_far * tq]] if n_far else []
            for j in range(n_far, i + 1):
                parts.append(s_t[j * tq:(j + 1) * tq] + bias_ref[0, i - j])
            m = functools.reduce(jnp.maximum, [jnp.max(part, axis=0, keepdims=True) for part in parts])
            out.append((parts, m))
        return out

    n_q = seq // tq
    queue = [scores(i) for i in range(min(ATT_PIPELINE_DEPTH, n_q))]
    for i in range(n_q):
        current = queue.pop(0)
        if i + ATT_PIPELINE_DEPTH < n_q:
            queue.append(scores(i + ATT_PIPELINE_DEPTH))
        n_keys = (i + 1) * tq
        probs = [jnp.concatenate([jnp.exp2(part - m).astype(BF16) for part in parts], axis=0) for parts, m in current]
        pv = _dot(v_aug_t[:, 0:n_keys], jnp.concatenate(probs, axis=1))
        num1, num2 = pv[0:ATT_DV, 0:tq], pv[0:ATT_DV, tq:]
        sum1, sum2 = pv[ATT_DV:ATT_DV + 1, 0:tq], pv[ATT_DV:ATT_DV + 1, tq:]
        out_t = num1 * (1.0 / sum1) - num2 * (lam / sum2)
        rms = lax.rsqrt(jnp.mean(out_t * out_t, axis=0, keepdims=True) + LN_EPS)
        o_ref[0, i * tq:(i + 1) * tq, :] = ((out_t * rms).T * g_ref[...] * (1.0 - lam_init)).astype(o_ref.dtype)


def _diff_attention(qkv, bias_tiles, lqk, subln_g, *, heads, tq, lam_init):
    b, s, _ = qkv.shape
    nb = bias_tiles.shape[1]
    live_q_blocks = 4
    est = 2 * 4 * s * ATT_DV * 2 + 2 * nb * tq * tq * 4 + live_q_blocks * 2 * 2 * s * tq * 4
    kern = functools.partial(_attn_kernel, tq=tq, lam_init=lam_init)
    return pl.pallas_call(
        kern,
        grid=(heads, b),
        in_specs=[pl.BlockSpec((1, s, ATT_DV), lambda h, bi: (bi, 0, h)),
                  pl.BlockSpec((1, s, ATT_DV), lambda h, bi: (bi, 0, heads + h)),
                  pl.BlockSpec((1, s, ATT_DV), lambda h, bi: (bi, 0, 2 * heads + h)),
                  pl.BlockSpec((1, nb, tq, tq), lambda h, bi: (h, 0, 0, 0)),
                  pl.BlockSpec(lqk.shape, lambda h, bi: (0, 0)),
                  pl.BlockSpec((1, ATT_DV), lambda h, bi: (0, 0))],
        out_specs=pl.BlockSpec((1, s, ATT_DV), lambda h, bi: (bi, 0, h)),
        out_shape=jax.ShapeDtypeStruct((b, s, heads * ATT_DV), BF16),
        compiler_params=pltpu.CompilerParams(
            dimension_semantics=("parallel", "parallel"), vmem_limit_bytes=_vmem_limit(est)),
        name="diff_attention",
    )(qkv, qkv, qkv, bias_tiles, lqk, subln_g)


def _out_proj_kernel(x_ref, rec_ref, att_ref, w_ref, g_ref, b_ref, o_ref, ob_ref, *, alpha):
    w1 = rec_ref.shape[1]
    m = _dot(rec_ref[...], w_ref[0:w1, :]) + _dot(att_ref[...], w_ref[w1:, :])
    h = _layer_norm(alpha * x_ref[...] + m, g_ref[...], b_ref[...])
    o_ref[...] = h
    ob_ref[...] = h.astype(BF16)


def _out_proj_ln(x2d, rec2d, att2d, w_bf16, g, b, alpha):
    m, d = x2d.shape
    w1, w2 = rec2d.shape[1], att2d.shape[1]
    tm = _tile(m, 512)
    est = 2 * tm * d * 4 * 2 + 2 * tm * (w1 + w2) * 2 + 2 * (w1 + w2) * d * 2 + 2 * tm * d * 4 + 2 * tm * d * 2
    rowblk = lambda i: (i, 0)
    fixed = lambda i: (0, 0)
    return pl.pallas_call(
        functools.partial(_out_proj_kernel, alpha=alpha),
        grid=(m // tm,),
        in_specs=[pl.BlockSpec((tm, d), rowblk), pl.BlockSpec((tm, w1), rowblk), pl.BlockSpec((tm, w2), rowblk),
                  pl.BlockSpec((w1 + w2, d), fixed), pl.BlockSpec((1, d), fixed), pl.BlockSpec((1, d), fixed)],
        out_specs=[pl.BlockSpec((tm, d), rowblk), pl.BlockSpec((tm, d), rowblk)],
        out_shape=[jax.ShapeDtypeStruct((m, d), F32), jax.ShapeDtypeStruct((m, d), BF16)],
        compiler_params=pltpu.CompilerParams(
            dimension_semantics=("parallel",), vmem_limit_bytes=_vmem_limit(est)),
        name="out_proj_ln1",
    )(x2d, rec2d, att2d, w_bf16, g, b)


def _ffn_kernel(hb_ref, wg_ref, wu_ref, wd_ref, o_ref):
    @pl.when(pl.program_id(1) == 0)
    def _():
        o_ref[...] = jnp.zeros(o_ref.shape, F32)

    hb = hb_ref[...]
    gate = _dot(hb, wg_ref[...].astype(BF16))
    up = _dot(hb, wu_ref[...].astype(BF16))
    act = (gate * _sigmoid_abs(gate)) * up
    o_ref[...] += _dot(act.astype(BF16), wd_ref[...].astype(BF16))


def _ffn(hb2d, wg, wu, wd):
    m, d = hb2d.shape
    f = wg.shape[1]
    tm, tf = _tile(m, 1024), _tile(f, 512)
    est = 2 * tm * d * 2 + 2 * 3 * d * tf * 4 + 2 * tm * d * 4 + 3 * tm * tf * 4 + tm * d * 4
    rowblk = lambda i, j: (i, 0)
    return pl.pallas_call(
        _ffn_kernel,
        grid=(m // tm, f // tf),
        in_specs=[pl.BlockSpec((tm, d), rowblk),
                  pl.BlockSpec((d, tf), lambda i, j: (0, j)),
                  pl.BlockSpec((d, tf), lambda i, j: (0, j)),
                  pl.BlockSpec((tf, d), lambda i, j: (j, 0))],
        out_specs=pl.BlockSpec((tm, d), rowblk),
        out_shape=jax.ShapeDtypeStruct((m, d), F32),
        compiler_params=pltpu.CompilerParams(
            dimension_semantics=("parallel", "arbitrary"), vmem_limit_bytes=_vmem_limit(est)),
        name="swiglu",
    )(hb2d, wg, wu, wd)


def _ple_kernel(h_ref, f_ref, p_ref, g2_ref, b2_ref, wg_ref, bg_ref, wp_ref, g3_ref, b3_ref, o_ref, *, alpha):
    h = _layer_norm(alpha * h_ref[...] + f_ref[...], g2_ref[...], b2_ref[...])
    gate = _sigmoid_abs(_dot(h.astype(BF16), wg_ref[...]) + bg_ref[...])
    e = _dot(p_ref[...].astype(BF16), wp_ref[...])
    o_ref[...] = _layer_norm(alpha * h + gate * e, g3_ref[...], b3_ref[...])


def _ln2_ple_ln3(h2d, f2d, p2d, g2, b2, wg, bg, wp, g3, b3, alpha):
    m, d = h2d.shape
    pd = p2d.shape[1]
    tm = _tile(m, 512)
    est = 3 * 2 * tm * d * 4 + 2 * tm * pd * 4 + 2 * d * d * 2 + 2 * pd * d * 2 + 3 * tm * d * 4
    rowblk = lambda i: (i, 0)
    fixed = lambda i: (0, 0)
    vec = pl.BlockSpec((1, d), fixed)
    return pl.pallas_call(
        functools.partial(_ple_kernel, alpha=alpha),
        grid=(m // tm,),
        in_specs=[pl.BlockSpec((tm, d), rowblk), pl.BlockSpec((tm, d), rowblk), pl.BlockSpec((tm, pd), rowblk),
                  vec, vec, pl.BlockSpec((d, d), fixed), vec, pl.BlockSpec((pd, d), fixed), vec, vec],
        out_specs=pl.BlockSpec((tm, d), rowblk),
        out_shape=jax.ShapeDtypeStruct((m, d), F32),
        compiler_params=pltpu.CompilerParams(
            dimension_semantics=("parallel",), vmem_limit_bytes=_vmem_limit(est)),
        name="ln2_ple_ln3",
    )(h2d, f2d, p2d, g2, b2, wg, bg, wp, g3, b3)


def kernel(x, p, w_in, conv_w, conv_b, lru_wa, lru_ba, lru_wx, lru_bx, lru_lambda, diff_lq1, diff_lk1, diff_lq2, diff_lk2, diff_subln_g, rel_bias, w_out, ln1_g, ln1_b, w_ffn_gate, w_ffn_up, w_ffn_down, ln2_g, ln2_b, w_ple_gate, b_ple_gate, w_ple_proj, ln3_g, ln3_b):
    batch, seq, d = x.shape
    depth = w_in.shape[0]
    nblk, bw = lru_wa.shape[1], lru_wa.shape[2]
    lru_w = nblk * bw
    heads = rel_bias.shape[1]
    att_qk = 2 * heads * ATT_DQK
    assert w_in.shape[2] == 2 * lru_w + 2 * att_qk + heads * ATT_DV
    assert lru_w % ATT_DV == 0 and att_qk == heads * ATT_DV
    m = batch * seq
    alpha = (2 * depth) ** 0.25

    tq = _tile(seq, 256)
    bias_tiles = _bias_tiles(rel_bias, tq)
    row = lambda v: v.reshape(1, -1)
    col = jnp.arange(2 * att_qk + heads * ATT_DV)
    qkv_scale = row(jnp.where(col < att_qk, ATT_DQK ** -0.5 * LOG2_E, 1.0).astype(F32))

    h = x.reshape(m, d)
    for i in range(depth):
        lam_init = 0.8 - 0.6 * math.exp(-0.3 * i)
        wgate = jnp.concatenate([lru_wa[i], lru_wx[i]], axis=-1).astype(BF16)
        qkv, rec = _in_proj_lru(h.reshape(batch, seq, d), w_in[i].astype(BF16), qkv_scale, conv_w[i], row(conv_b[i]),
                                wgate, row(lru_ba[i]), row(lru_bx[i]), row(lru_lambda[i]))
        lqk = jnp.stack([diff_lq1[i], diff_lk1[i], diff_lq2[i], diff_lk2[i]])
        att = _diff_attention(qkv, bias_tiles, lqk, row(diff_subln_g[i]), heads=heads, tq=tq, lam_init=lam_init)
        h, hb = _out_proj_ln(h, rec.reshape(m, lru_w), att.reshape(m, heads * ATT_DV), w_out[i].astype(BF16),
                             row(ln1_g[i]), row(ln1_b[i]), alpha)
        f = _ffn(hb, w_ffn_gate[i], w_ffn_up[i], w_ffn_down[i])
        h = _ln2_ple_ln3(h, f, p[i].reshape(m, -1), row(ln2_g[i]), row(ln2_b[i]), w_ple_gate[i].astype(BF16),
                         row(b_ple_gate[i]), w_ple_proj[i].astype(BF16), row(ln3_g[i]), row(ln3_b[i]), alpha)
    return h.reshape(batch, seq, d)
```

```python
import functools
import math

import jax
import jax.numpy as jnp
import numpy as np
from jax import lax
from jax.experimental import pallas as pl
from jax.experimental.pallas import tpu as pltpu

F32 = jnp.float32
BF16 = jnp.bfloat16

LN_EPS = 1e-5
LRU_C = 8.0
CONV_W = 4
ATT_DV = 128
ATT_DQK = ATT_DV // 2
REL_BUCKETS = 32
REL_MAX_DIST = 128
ATT_PIPELINE_DEPTH = 2
LOG2_E = math.log2(math.e)

V7X_LANES = 128
V7X_SUBLANES = 8
BF16_SUBLANES = 2 * V7X_SUBLANES
V7X_VMEM_BYTES = 64 * 1024 * 1024
V7X_VMEM_RESERVE_BYTES = 6 * 1024 * 1024


def _vmem_limit(estimate_bytes):
    return int(min(V7X_VMEM_BYTES - V7X_VMEM_RESERVE_BYTES, max(2 * estimate_bytes, 16 * 1024 * 1024)))


def _tile(dim, preferred):
    if dim % preferred == 0:
        return preferred
    return dim


def _layer_norm(y, g, b):
    mu = jnp.mean(y, axis=-1, keepdims=True)
    yc = y - mu
    var = jnp.mean(yc * yc, axis=-1, keepdims=True)
    return yc * lax.rsqrt(var + LN_EPS) * g + b


def _dot(a, b):
    return jnp.dot(a, b, preferred_element_type=F32)


def _gelu_tanh(x):
    return 0.5 * x * (1.0 + jnp.tanh(math.sqrt(2.0 / math.pi) * (x + 0.044715 * (x * x * x))))


def _sigmoid_abs(x):
    return 0.5 * jnp.tanh(0.5 * x) + 0.5


def _in_proj_lru_kernel(x_ref, w_ref, cs_ref, cw_ref, cb_ref, wg_ref, ba_ref, bx_ref, lam_ref,
                        qkv_ref, rec_ref, xext_ref, a_ref, u_ref, h_ref, *, chunk):
    nbatch, tt, d = x_ref.shape
    rows = nbatch * tt
    lru_w = rec_ref.shape[2]
    n_qkv = qkv_ref.shape[2]
    bw = wg_ref.shape[1]
    halo = V7X_SUBLANES
    pitch = xext_ref.shape[1]
    half = V7X_SUBLANES

    @pl.when(pl.program_id(0) == 0)
    def _():
        xext_ref[:, 0:halo, :] = jnp.zeros((nbatch, halo, lru_w), F32)
        h_ref[...] = jnp.zeros(h_ref.shape, F32)

    xb = x_ref[...].reshape(rows, d).astype(BF16)
    xext_ref[:, halo:halo + tt, :] = _dot(xb, w_ref[:, 0:lru_w]).reshape(nbatch, tt, lru_w)
    yg = _dot(xb, w_ref[:, lru_w:2 * lru_w])
    for c0 in range(0, n_qkv, chunk):
        val = _dot(xb, w_ref[:, 2 * lru_w + c0:2 * lru_w + c0 + chunk]) * cs_ref[:, c0:c0 + chunk]
        qkv_ref[:, :, c0:c0 + chunk] = val.astype(BF16).reshape(nbatch, tt, chunk)

    for n in range(lru_w // bw):
        cols = slice(n * bw, (n + 1) * bw)
        xc = cb_ref[:, cols]
        for t in range(CONV_W):
            start = halo - (CONV_W - 1) + t
            xc = xc + xext_ref[:, start:start + tt, cols].reshape(rows, bw) * cw_ref[t:t + 1, cols]
        gates = _dot(xc.astype(BF16), wg_ref[n])
        gate_a = jax.nn.sigmoid(gates[:, :bw] + ba_ref[:, cols])
        gate_x = _sigmoid_abs(gates[:, bw:] + bx_ref[:, cols])
        neg_lam = -lam_ref[:, cols]
        softplus = jnp.maximum(neg_lam, 0.0) + jnp.log1p(jnp.exp(-jnp.abs(neg_lam)))
        log_a = gate_a * (-LRU_C * softplus)
        a = jnp.exp(log_a)
        one_minus_a2 = -jnp.tanh(log_a) * (a * a + 1.0)
        mult = jnp.where(one_minus_a2 > 0.0, one_minus_a2 * lax.rsqrt(one_minus_a2), 0.0)
        u = mult * (gate_x * xc)
        for bi in range(nbatch):
            a_ref[n, bi * pitch:bi * pitch + tt, :] = a[bi * tt:(bi + 1) * tt]
            u_ref[n, bi * pitch:bi * pitch + tt, :] = u[bi * tt:(bi + 1) * tt]

        for g in range(nbatch // half):
            h = h_ref[g * half:(g + 1) * half, cols]
            for t in range(tt):
                step = pl.ds(g * half * pitch + t, half, stride=pitch)
                h = a_ref[n, step, :] * h + u_ref[n, step, :]
                u_ref[n, step, :] = h
            h_ref[g * half:(g + 1) * half, cols] = h

        hs = jnp.concatenate([u_ref[n, bi * pitch:bi * pitch + tt, :] for bi in range(nbatch)], axis=0)
        rec_ref[:, :, cols] = (hs * _gelu_tanh(yg[:, cols])).astype(BF16).reshape(nbatch, tt, bw)

    xext_ref[:, 0:halo, :] = xext_ref[:, tt:tt + halo, :]


def _in_proj_lru(x, w_bf16, qkv_scale, conv_w, conv_b, wgate, ba, bx, lam):
    nbatch, seq, d = x.shape
    nblk, bw, _ = wgate.shape
    lru_w = nblk * bw
    n_in = w_bf16.shape[1]
    n_qkv = n_in - 2 * lru_w
    tt = _tile(seq, 32)
    rows = nbatch * tt
    pitch = tt + V7X_SUBLANES
    chunk = _tile(n_qkv, 1024)
    assert nbatch % V7X_SUBLANES == 0 and tt % BF16_SUBLANES == 0
    est = (d * n_in * 2 + 2 * rows * d * 4 + 2 * rows * (n_qkv + lru_w) * 2 + 3 * nbatch * pitch * lru_w * 4
           + rows * d * 2 + 3 * rows * lru_w * 4 + 2 * rows * chunk * 4)
    fixed = lambda t: (0, 0)
    kern = functools.partial(_in_proj_lru_kernel, chunk=chunk)
    return pl.pallas_call(
        kern,
        grid=(seq // tt,),
        in_specs=[pl.BlockSpec((nbatch, tt, d), lambda t: (0, t, 0)),
                  pl.BlockSpec((d, n_in), fixed, pipeline_mode=pl.Buffered(1)),
                  pl.BlockSpec((1, n_qkv), fixed),
                  pl.BlockSpec((CONV_W, lru_w), fixed),
                  pl.BlockSpec((1, lru_w), fixed),
                  pl.BlockSpec((nblk, bw, 2 * bw), lambda t: (0, 0, 0)),
                  pl.BlockSpec((1, lru_w), fixed),
                  pl.BlockSpec((1, lru_w), fixed),
                  pl.BlockSpec((1, lru_w), fixed)],
        out_specs=[pl.BlockSpec((nbatch, tt, n_qkv), lambda t: (0, t, 0)),
                   pl.BlockSpec((nbatch, tt, lru_w), lambda t: (0, t, 0))],
        out_shape=[jax.ShapeDtypeStruct((nbatch, seq, n_qkv), BF16),
                   jax.ShapeDtypeStruct((nbatch, seq, lru_w), BF16)],
        scratch_shapes=[pltpu.VMEM((nbatch, pitch, lru_w), F32),
                        pltpu.VMEM((nblk, nbatch * pitch, bw), F32),
                        pltpu.VMEM((nblk, nbatch * pitch, bw), F32),
                        pltpu.VMEM((nbatch, lru_w), F32)],
        compiler_params=pltpu.CompilerParams(
            dimension_semantics=("arbitrary",), vmem_limit_bytes=_vmem_limit(est)),
        name="in_proj_rg_lru",
    )(x, w_bf16, qkv_scale, conv_w, conv_b, wgate, ba, bx, lam)


def _bucket_thresholds():
    n = np.arange(0, 8 * REL_MAX_DIST)
    max_exact = REL_BUCKETS // 2
    nf = np.maximum(n, 1).astype(np.float32)
    large = max_exact + (np.log(nf / np.float32(max_exact)) / np.float32(math.log(REL_MAX_DIST / max_exact))
                         * np.float32(REL_BUCKETS - max_exact)).astype(np.int32)
    bucket = np.where(n < max_exact, n, np.minimum(large, REL_BUCKETS - 1))
    assert np.all(np.diff(bucket) >= 0) and bucket[-1] == REL_BUCKETS - 1
    return [int(np.argmax(bucket >= j)) for j in range(1, REL_BUCKETS)]


def _near_key_blocks(tq):
    last = _bucket_thresholds()[-1]
    d = 0
    while d * tq - (tq - 1) < last:
        d += 1
    return d


def _bias_kernel(table_ref, o_ref, *, thresholds):
    h = pl.program_id(0)
    nb, tk, tq = o_ref.shape[1], o_ref.shape[2], o_ref.shape[3]
    key = lax.broadcasted_iota(jnp.int32, (tk, tq), 0)
    qry = lax.broadcasted_iota(jnp.int32, (tk, tq), 1)
    far = table_ref[REL_BUCKETS - 1, h]
    for d in range(nb):
        dist = d * tk + qry - key
        val = jnp.full((tk, tq), table_ref[0, h], F32)
        for j, thr in enumerate(thresholds):
            val = jnp.where(dist >= thr, table_ref[j + 1, h], val)
        o_ref[0, d] = jnp.where(dist < 0, -jnp.inf, (val - far) * LOG2_E)


def _bias_tiles(rel_bias, tq):
    heads = rel_bias.shape[1]
    nb = _near_key_blocks(tq)
    return pl.pallas_call(
        functools.partial(_bias_kernel, thresholds=_bucket_thresholds()),
        grid=(heads,),
        in_specs=[pl.BlockSpec(memory_space=pltpu.SMEM)],
        out_specs=pl.BlockSpec((1, nb, tq, tq), lambda h: (h, 0, 0, 0)),
        out_shape=jax.ShapeDtypeStruct((heads, nb, tq, tq), F32),
        compiler_params=pltpu.CompilerParams(
            dimension_semantics=("parallel",), vmem_limit_bytes=_vmem_limit(2 * nb * tq * tq * 4)),
        name="rel_bias_tiles",
    )(rel_bias)


def _attn_kernel(q_ref, k_ref, v_ref, bias_ref, lqk_ref, g_ref, o_ref, *, tq, lam_init):
    seq = q_ref.shape[1]
    nb = bias_ref.shape[1]
    nt_dims = (((1,), (1,)), ((), ()))

    lqk = lqk_ref[...]
    lam = (jnp.exp(jnp.sum(lqk[0:1] * lqk[1:2], axis=-1, keepdims=True))
           - jnp.exp(jnp.sum(lqk[2:3] * lqk[3:4], axis=-1, keepdims=True)) + lam_init)
    ones_rows = (lax.broadcasted_iota(jnp.int32, (BF16_SUBLANES, seq), 0) == 0).astype(F32)
    v_aug_t = jnp.concatenate([v_ref[0].astype(F32).T, ones_rows], axis=0).astype(BF16)
    lane = lax.broadcasted_iota(jnp.int32, (tq, ATT_DV), 1)

    def scores(i):
        q = q_ref[0, i * tq:(i + 1) * tq, :]
        k = k_ref[0, 0:(i + 1) * tq, :]
        n_far = max(i + 1 - nb, 0)
        out = []
        for c in range(2):
            q_c = jnp.where((lane >= ATT_DQK) if c else (lane < ATT_DQK), q, jnp.zeros_like(q))
            s_t = lax.dot_general(k, q_c, nt_dims, preferred_element_type=F32)
            parts = [s_t[0:n_far * tq]] if n_far else []
            for j in range(n_far, i + 1):
                parts.append(s_t[j * tq:(j + 1) * tq] + bias_ref[0, i - j])
            m = functools.reduce(jnp.maximum, [jnp.max(part, axis=0, keepdims=True) for part in parts])
            out.append((parts, m))
        return out

    n_q = seq // tq
    queue = [scores(i) for i in range(min(ATT_PIPELINE_DEPTH, n_q))]
    for i in range(n_q):
        current = queue.pop(0)
        if i + ATT_PIPELINE_DEPTH < n_q:
            queue.append(scores(i + ATT_PIPELINE_DEPTH))
        n_keys = (i + 1) * tq
        probs = [jnp.concatenate([jnp.exp2(part - m).astype(BF16) for part in parts], axis=0) for parts, m in current]
        pv = _dot(v_aug_t[:, 0:n_keys], jnp.concatenate(probs, axis=1))
        num1, num2 = pv[0:ATT_DV, 0:tq], pv[0:ATT_DV, tq:]
        sum1, sum2 = pv[ATT_DV:ATT_DV + 1, 0:tq], pv[ATT_DV:ATT_DV + 1, tq:]
        out_t = num1 * (1.0 / sum1) - num2 * (lam / sum2)
        rms = lax.rsqrt(jnp.mean(out_t * out_t, axis=0, keepdims=True) + LN_EPS)
        o_ref[0, i * tq:(i + 1) * tq, :] = ((out_t * rms).T * g_ref[...] * (1.0 - lam_init)).astype(o_ref.dtype)


def _diff_attention(qkv, bias_tiles, lqk, subln_g, *, heads, tq, lam_init):
    b, s, _ = qkv.shape
    nb = bias_tiles.shape[1]
    live_q_blocks = 4
    est = 2 * 4 * s * ATT_DV * 2 + 2 * nb * tq * tq * 4 + live_q_blocks * 2 * 2 * s * tq * 4
    kern = functools.partial(_attn_kernel, tq=tq, lam_init=lam_init)
    return pl.pallas_call(
        kern,
        grid=(heads, b),
        in_specs=[pl.BlockSpec((1, s, ATT_DV), lambda h, bi: (bi, 0, h)),
                  pl.BlockSpec((1, s, ATT_DV), lambda h, bi: (bi, 0, heads + h)),
                  pl.BlockSpec((1, s, ATT_DV), lambda h, bi: (bi, 0, 2 * heads + h)),
                  pl.BlockSpec((1, nb, tq, tq), lambda h, bi: (h, 0, 0, 0)),
                  pl.BlockSpec(lqk.shape, lambda h, bi: (0, 0)),
                  pl.BlockSpec((1, ATT_DV), lambda h, bi: (0, 0))],
        out_specs=pl.BlockSpec((1, s, ATT_DV), lambda h, bi: (bi, 0, h)),
        out_shape=jax.ShapeDtypeStruct((b, s, heads * ATT_DV), BF16),
        compiler_params=pltpu.CompilerParams(
            dimension_semantics=("parallel", "parallel"), vmem_limit_bytes=_vmem_limit(est)),
        name="diff_attention",
    )(qkv, qkv, qkv, bias_tiles, lqk, subln_g)


def _out_proj_kernel(x_ref, rec_ref, att_ref, w_ref, g_ref, b_ref, o_ref, ob_ref, *, alpha):
    w1 = rec_ref.shape[1]
    m = _dot(rec_ref[...], w_ref[0:w1, :]) + _dot(att_ref[...], w_ref[w1:, :])
    h = _layer_norm(alpha * x_ref[...] + m, g_ref[...], b_ref[...])
    o_ref[...] = h
    ob_ref[...] = h.astype(BF16)


def _out_proj_ln(x2d, rec2d, att2d, w_bf16, g, b, alpha):
    m, d = x2d.shape
    w1, w2 = rec2d.shape[1], att2d.shape[1]
    tm = _tile(m, 512)
    est = 2 * tm * d * 4 * 2 + 2 * tm * (w1 + w2) * 2 + 2 * (w1 + w2) * d * 2 + 2 * tm * d * 4 + 2 * tm * d * 2
    rowblk = lambda i: (i, 0)
    fixed = lambda i: (0, 0)
    return pl.pallas_call(
        functools.partial(_out_proj_kernel, alpha=alpha),
        grid=(m // tm,),
        in_specs=[pl.BlockSpec((tm, d), rowblk), pl.BlockSpec((tm, w1), rowblk), pl.BlockSpec((tm, w2), rowblk),
                  pl.BlockSpec((w1 + w2, d), fixed), pl.BlockSpec((1, d), fixed), pl.BlockSpec((1, d), fixed)],
        out_specs=[pl.BlockSpec((tm, d), rowblk), pl.BlockSpec((tm, d), rowblk)],
        out_shape=[jax.ShapeDtypeStruct((m, d), F32), jax.ShapeDtypeStruct((m, d), BF16)],
        compiler_params=pltpu.CompilerParams(
            dimension_semantics=("parallel",), vmem_limit_bytes=_vmem_limit(est)),
        name="out_proj_ln1",
    )(x2d, rec2d, att2d, w_bf16, g, b)


def _ffn_kernel(hb_ref, wg_ref, wu_ref, wd_ref, o_ref):
    @pl.when(pl.program_id(1) == 0)
    def _():
        o_ref[...] = jnp.zeros(o_ref.shape, F32)

    hb = hb_ref[...]
    gate = _dot(hb, wg_ref[...].astype(BF16))
    up = _dot(hb, wu_ref[...].astype(BF16))
    act = (gate * _sigmoid_abs(gate)) * up
    o_ref[...] += _dot(act.astype(BF16), wd_ref[...].astype(BF16))


def _ffn(hb2d, wg, wu, wd):
    m, d = hb2d.shape
    f = wg.shape[1]
    tm, tf = _tile(m, 1024), _tile(f, 512)
    est = 2 * tm * d * 2 + 2 * 3 * d * tf * 4 + 2 * tm * d * 4 + 3 * tm * tf * 4 + tm * d * 4
    rowblk = lambda i, j: (i, 0)
    return pl.pallas_call(
        _ffn_kernel,
        grid=(m // tm, f // tf),
        in_specs=[pl.BlockSpec((tm, d), rowblk),
                  pl.BlockSpec((d, tf), lambda i, j: (0, j)),
                  pl.BlockSpec((d, tf), lambda i, j: (0, j)),
                  pl.BlockSpec((tf, d), lambda i, j: (j, 0))],
        out_specs=pl.BlockSpec((tm, d), rowblk),
        out_shape=jax.ShapeDtypeStruct((m, d), F32),
        compiler_params=pltpu.CompilerParams(
            dimension_semantics=("parallel", "arbitrary"), vmem_limit_bytes=_vmem_limit(est)),
        name="swiglu",
    )(hb2d, wg, wu, wd)


def _ple_kernel(h_ref, f_ref, p_ref, g2_ref, b2_ref, wg_ref, bg_ref, wp_ref, g3_ref, b3_ref, o_ref, *, alpha):
    h = _layer_norm(alpha * h_ref[...] + f_ref[...], g2_ref[...], b2_ref[...])
    gate = _sigmoid_abs(_dot(h.astype(BF16), wg_ref[...]) + bg_ref[...])
    e = _dot(p_ref[...].astype(BF16), wp_ref[...])
    o_ref[...] = _layer_norm(alpha * h + gate * e, g3_ref[...], b3_ref[...])


def _ln2_ple_ln3(h2d, f2d, p2d, g2, b2, wg, bg, wp, g3, b3, alpha):
    m, d = h2d.shape
    pd = p2d.shape[1]
    tm = _tile(m, 512)
    est = 3 * 2 * tm * d * 4 + 2 * tm * pd * 4 + 2 * d * d * 2 + 2 * pd * d * 2 + 3 * tm * d * 4
    rowblk = lambda i: (i, 0)
    fixed = lambda i: (0, 0)
    vec = pl.BlockSpec((1, d), fixed)
    return pl.pallas_call(
        functools.partial(_ple_kernel, alpha=alpha),
        grid=(m // tm,),
        in_specs=[pl.BlockSpec((tm, d), rowblk), pl.BlockSpec((tm, d), rowblk), pl.BlockSpec((tm, pd), rowblk),
                  vec, vec, pl.BlockSpec((d, d), fixed), vec, pl.BlockSpec((pd, d), fixed), vec, vec],
        out_specs=pl.BlockSpec((tm, d), rowblk),
        out_shape=jax.ShapeDtypeStruct((m, d), F32),
        compiler_params=pltpu.CompilerParams(
            dimension_semantics=("parallel",), vmem_limit_bytes=_vmem_limit(est)),
        name="ln2_ple_ln3",
    )(h2d, f2d, p2d, g2, b2, wg, bg, wp, g3, b3)


def kernel(x, p, w_in, conv_w, conv_b, lru_wa, lru_ba, lru_wx, lru_bx, lru_lambda, diff_lq1, diff_lk1, diff_lq2, diff_lk2, diff_subln_g, rel_bias, w_out, ln1_g, ln1_b, w_ffn_gate, w_ffn_up, w_ffn_down, ln2_g, ln2_b, w_ple_gate, b_ple_gate, w_ple_proj, ln3_g, ln3_b):
    batch, seq, d = x.shape
    depth = w_in.shape[0]
    nblk, bw = lru_wa.shape[1], lru_wa.shape[2]
    lru_w = nblk * bw
    heads = rel_bias.shape[1]
    att_qk = 2 * heads * ATT_DQK
    assert w_in.shape[2] == 2 * lru_w + 2 * att_qk + heads * ATT_DV
    assert lru_w % ATT_DV == 0 and att_qk == heads * ATT_DV
    m = batch * seq
    alpha = (2 * depth) ** 0.25

    tq = _tile(seq, 256)
    bias_tiles = _bias_tiles(rel_bias, tq)
    row = lambda v: v.reshape(1, -1)
    col = jnp.arange(2 * att_qk + heads * ATT_DV)
    qkv_scale = row(jnp.where(col < att_qk, ATT_DQK ** -0.5 * LOG2_E, 1.0).astype(F32))

    h = x.reshape(m, d)
    for i in range(depth):
        lam_init = 0.8 - 0.6 * math.exp(-0.3 * i)
        wgate = jnp.concatenate([lru_wa[i], lru_wx[i]], axis=-1).astype(BF16)
        qkv, rec = _in_proj_lru(h.reshape(batch, seq, d), w_in[i].astype(BF16), qkv_scale, conv_w[i], row(conv_b[i]),
                                wgate, row(lru_ba[i]), row(lru_bx[i]), row(lru_lambda[i]))
        lqk = jnp.stack([diff_lq1[i], diff_lk1[i], diff_lq2[i], diff_lk2[i]])
        att = _diff_attention(qkv, bias_tiles, lqk, row(diff_subln_g[i]), heads=heads, tq=tq, lam_init=lam_init)
        h, hb = _out_proj_ln(h, rec.reshape(m, lru_w), att.reshape(m, heads * ATT_DV), w_out[i].astype(BF16),
                             row(ln1_g[i]), row(ln1_b[i]), alpha)
        f = _ffn(hb, w_ffn_gate[i], w_ffn_up[i], w_ffn_down[i])
        h = _ln2_ple_ln3(h, f, p[i].reshape(m, -1), row(ln2_g[i]), row(ln2_b[i]), w_ple_gate[i].astype(BF16),
                         row(b_ple_gate[i]), w_ple_proj[i].astype(BF16), row(ln3_g[i]), row(ln3_b[i]), alpha)
    return h.reshape(batch, seq, d)
```
